```python
import jax, jax.numpy as jnp
from jax import lax
import numpy as np

D_MODEL = 4096
BATCH = 2
SEQ = 4096
DEPTH = 2

CHUNK = 64
Q_BLOCK = 128
HEAD_DIM = 128
N_MIX_HEADS = D_MODEL // HEAD_DIM
A_HEADS = N_MIX_HEADS // 2
LEFT_CHUNKS = 8
BAND = (LEFT_CHUNKS + 1) * CHUNK
REL_CLIP = 256
B_HEADS = N_MIX_HEADS // 2
MLA_Q_LORA = 1536
MLA_KV_LORA = 512
MLA_NOPE = 128
MLA_ROPE = 64
MLA_V = 128
C_HEADS = N_MIX_HEADS // 2
C_KV_HEADS = 4
IDX_HEADS = 32
IDX_HD = 64
DSA_TOPK_MAX = 256
D_HEADS = N_MIX_HEADS // 2
FOX_FORGET_BIAS = 2.0
MEM_TOKENS = 256
MEM_HEADS = 4
MEM_HD = 128
PEER_HEADS = 8
PEER_DKEY = 256
N_KEYS = 128
N_EXPERTS = N_KEYS * N_KEYS
PEER_TOPK = 16
PEER_BLOCK = 64
ROPE_THETA = 10000.0
LN_EPS = 1e-5
RMS_EPS = 1e-6
DEEPNORM_ALPHA = (2 * DEPTH) ** 0.25
DEEPNORM_BETA = (8 * DEPTH) ** -0.25
N_EVEN = (DEPTH + 1) // 2
N_ODD = DEPTH // 2

A_W = A_HEADS * HEAD_DIM
AB_SPLITS = [A_W, A_W, A_W, MLA_Q_LORA, MLA_KV_LORA, MLA_ROPE]
AB_IN_COLS = sum(AB_SPLITS)
AB_MIX = A_HEADS * HEAD_DIM + B_HEADS * MLA_V
C_QW = C_HEADS * HEAD_DIM
C_KW = C_KV_HEADS * HEAD_DIM
D_W = D_HEADS * HEAD_DIM
CD_SPLITS = [C_QW, C_KW, C_KW, IDX_HEADS * IDX_HD, IDX_HD, IDX_HEADS, D_W, D_W, D_W, D_HEADS]
CD_IN_COLS = sum(CD_SPLITS)
CD_MIX = C_QW + D_W
MEM_W = MEM_HEADS * MEM_HD

kernel_name = 'hybrid_chunk_streaming_encoder'


def split_cols(h, sizes):
    offs = [int(o) for o in np.cumsum(sizes)[:-1]]
    return jnp.split(h, offs, axis=-1)


def layer_norm(x, g, b):
    xf = x.astype(jnp.float32)
    xc = xf - jnp.mean(xf, -1, keepdims=True)
    var = jnp.mean(xc * xc, -1, keepdims=True)
    return (xc * lax.rsqrt(var + LN_EPS) * g + b).astype(x.dtype)


def rms_norm(x, g):
    xf = x.astype(jnp.float32)
    return (xf * lax.rsqrt(jnp.mean(xf * xf, -1, keepdims=True) + RMS_EPS) * g).astype(x.dtype)


def rope(x):
    S, dim = x.shape[1], x.shape[-1]
    half = dim // 2
    inv = ROPE_THETA ** (-jnp.arange(half, dtype=jnp.float32) * 2.0 / dim)
    ang = jnp.arange(S, dtype=jnp.float32)[:, None] * inv[None, :]
    cos = jnp.cos(ang)[None, :, None, :]
    sin = jnp.sin(ang)[None, :, None, :]
    xf = x.astype(jnp.float32)
    x1, x2 = xf[..., :half], xf[..., half:]
    return jnp.concatenate([x1 * cos - x2 * sin, x2 * cos + x1 * sin], -1).astype(x.dtype)


def swept_attention(q, k, v, frame_causal, log_decay_cum=None):
    B, S, H, dk = q.shape
    scale = dk ** -0.5
    kpos = jnp.arange(S)
    ct = None if log_decay_cum is None else jnp.swapaxes(log_decay_cum, 1, 2)

    def block(i):
        start = i * Q_BLOCK
        qb = lax.dynamic_slice_in_dim(q, start, Q_BLOCK, axis=1)
        qpos = start + jnp.arange(Q_BLOCK)
        s = jnp.einsum('bqhd,bkhd->bhqk', qb, k, preferred_element_type=jnp.float32) * scale
        if frame_causal:
            mask = kpos[None, :] <= qpos[:, None]
        else:
            mask = (kpos[None, :] // CHUNK) <= (qpos[:, None] // CHUNK)
        if ct is not None:
            cq = lax.dynamic_slice_in_dim(ct, start, Q_BLOCK, axis=2)
            s = s + cq[..., :, None] - ct[:, :, None, :]
        p = jax.nn.softmax(jnp.where(mask, s, -jnp.inf), axis=-1)
        return jnp.einsum('bhqk,bkhd->bqhd', p.astype(v.dtype), v)

    out = lax.map(block, jnp.arange(S // Q_BLOCK))
    return jnp.swapaxes(out, 0, 1).reshape(B, S, H, v.shape[-1])


def chunk_band_attention(q, k, v, rel_bias):
    B, S, H, d = q.shape
    pad = LEFT_CHUNKS * CHUNK
    kp = jnp.pad(k, ((0, 0), (pad, 0), (0, 0), (0, 0)))
    vp = jnp.pad(v, ((0, 0), (pad, 0), (0, 0), (0, 0)))
    rel = jnp.arange(CHUNK)[:, None] + pad - jnp.arange(BAND)[None, :]
    bias = rel_bias[:, jnp.clip(rel, -REL_CLIP, REL_CLIP) + REL_CLIP].astype(jnp.float32)
    scale = d ** -0.5

    def block(c):
        qb = lax.dynamic_slice_in_dim(q, c * CHUNK, CHUNK, axis=1)
        kb = lax.dynamic_slice_in_dim(kp, c * CHUNK, BAND, axis=1)
        vb = lax.dynamic_slice_in_dim(vp, c * CHUNK, BAND, axis=1)
        s = jnp.einsum('bqhd,bkhd->bhqk', qb, kb, preferred_element_type=jnp.float32) * scale + bias
        valid = (c * CHUNK - pad + jnp.arange(BAND)) >= 0
        p = jax.nn.softmax(jnp.where(valid, s, -jnp.inf), axis=-1)
        return jnp.einsum('bhqk,bkhd->bqhd', p.astype(vb.dtype), vb)

    out = lax.map(block, jnp.arange(S // CHUNK))
    return jnp.swapaxes(out, 0, 1).reshape(B, S, H, d)


def dsa_attention(q, k, v, qi, ki, wi):
    B, S, H, d = q.shape
    G = k.shape[2]
    R = H // G
    topk = min(DSA_TOPK_MAX, S // 4)
    scale = d ** -0.5
    kchunk = jnp.arange(S) // CHUNK
    wi = wi.astype(jnp.float32) * IDX_HEADS ** -0.5
    gather = jax.vmap(lambda t, ii: t[ii])

    def block(i):
        start = i * Q_BLOCK
        qb = lax.dynamic_slice_in_dim(q, start, Q_BLOCK, axis=1)
        qib = lax.dynamic_slice_in_dim(qi, start, Q_BLOCK, axis=1)
        wib = lax.dynamic_slice_in_dim(wi, start, Q_BLOCK, axis=1)
        qchunk = (start + jnp.arange(Q_BLOCK)) // CHUNK
        idx_logit = jnp.einsum('bqhd,bkd->bqhk', qib, ki, preferred_element_type=jnp.float32) * IDX_HD ** -0.5
        score = jnp.einsum('bqh,bqhk->bqk', wib, jax.nn.relu(idx_logit))
        admissible = kchunk[None, :] <= qchunk[:, None]
        _, sel = lax.top_k(jnp.where(admissible, score, -jnp.inf), topk)
        sel_ok = kchunk[sel] <= qchunk[None, :, None]
        k_sel = gather(k, sel)
        v_sel = gather(v, sel)
        s = jnp.einsum('bqgrd,bqkgd->bqgrk', qb.reshape(B, Q_BLOCK, G, R, d), k_sel,
                       preferred_element_type=jnp.float32) * scale
        p = jax.nn.softmax(jnp.where(sel_ok[:, :, None, None, :], s, -jnp.inf), axis=-1)
        o = jnp.einsum('bqgrk,bqkgd->bqgrd', p.astype(v.dtype), v_sel)
        return o.reshape(B, Q_BLOCK, H, d)

    out = lax.map(block, jnp.arange(S // Q_BLOCK))
    return jnp.swapaxes(out, 0, 1).reshape(B, S, H, d)


def mixer_ab(x, w_in, rel_bias, q_norm, w_uq, kv_norm, w_ukv, w_out):
    B, S, _ = x.shape
    qa, ka, va, cq, ckv, kr = split_cols(x @ w_in, AB_SPLITS)
    oa = chunk_band_attention(qa.reshape(B, S, A_HEADS, HEAD_DIM), ka.reshape(B, S, A_HEADS, HEAD_DIM),
                              va.reshape(B, S, A_HEADS, HEAD_DIM), rel_bias)
    qb = (rms_norm(cq, q_norm) @ w_uq).reshape(B, S, B_HEADS, MLA_NOPE + MLA_ROPE)
    qb = jnp.concatenate([qb[..., :MLA_NOPE], rope(qb[..., MLA_NOPE:])], -1)
    kv = (rms_norm(ckv, kv_norm) @ w_ukv).reshape(B, S, B_HEADS, MLA_NOPE + MLA_V)
    k_rope = jnp.broadcast_to(rope(kr[:, :, None, :]), (B, S, B_HEADS, MLA_ROPE))
    kb = jnp.concatenate([kv[..., :MLA_NOPE], k_rope], -1)
    ob = swept_attention(qb, kb, kv[..., MLA_NOPE:], frame_causal=False)
    o = jnp.concatenate([oa.reshape(B, S, -1), ob.reshape(B, S, -1)], -1)
    return o @ w_out


def mixer_cd(x, w_in, forget_bias, w_out):
    B, S, _ = x.shape
    qc, kc, vc, qi, ki, wi, qd, kd, vd, fd = split_cols(x @ w_in, CD_SPLITS)
    oc = dsa_attention(rope(qc.reshape(B, S, C_HEADS, HEAD_DIM)),
                       rope(kc.reshape(B, S, C_KV_HEADS, HEAD_DIM)),
                       vc.reshape(B, S, C_KV_HEADS, HEAD_DIM),
                       rope(qi.reshape(B, S, IDX_HEADS, IDX_HD)),
                       rope(ki[:, :, None, :])[:, :, 0, :],
                       wi)
    log_f = jax.nn.log_sigmoid(fd.astype(jnp.float32) + forget_bias.astype(jnp.float32))
    cum = lax.cumsum(log_f, axis=1)
    od = swept_attention(qd.reshape(B, S, D_HEADS, HEAD_DIM), kd.reshape(B, S, D_HEADS, HEAD_DIM),
                         vd.reshape(B, S, D_HEADS, HEAD_DIM), frame_causal=True, log_decay_cum=cum)
    o = jnp.concatenate([oc.reshape(B, S, -1), od.reshape(B, S, -1)], -1)
    return o @ w_out


def memory_cross_attention(x, mem, w_q, w_kv, w_o):
    B, S, _ = x.shape
    M = mem.shape[1]
    q = (x @ w_q).reshape(B, S, MEM_HEADS, MEM_HD)
    kv = (mem @ w_kv).reshape(B, M, 2, MEM_HEADS, MEM_HD)
    s = jnp.einsum('bshd,bmhd->bhsm', q, kv[:, :, 0], preferred_element_type=jnp.float32) * MEM_HD ** -0.5
    p = jax.nn.softmax(s, axis=-1)
    o = jnp.einsum('bhsm,bmhd->bshd', p.astype(kv.dtype), kv[:, :, 1])
    return o.reshape(B, S, MEM_W) @ w_o


def peer_ffn(x, w_q, sub_keys, u_tab, v_tab):
    B, S, D = x.shape
    q = (x @ w_q).reshape(B, S, PEER_HEADS, 2, PEER_DKEY // 2)
    s = jnp.einsum('bshpd,hpnd->bshpn', q, sub_keys, preferred_element_type=jnp.float32)
    top_s, top_i = lax.top_k(s, PEER_TOPK)
    cand_s = (top_s[..., 0, :, None] + top_s[..., 1, None, :]).reshape(B, S, PEER_HEADS, PEER_TOPK * PEER_TOPK)
    cand_i = (top_i[..., 0, :, None] * N_KEYS + top_i[..., 1, None, :]).reshape(B, S, PEER_HEADS, PEER_TOPK * PEER_TOPK)
    best_s, best_j = lax.top_k(cand_s, PEER_TOPK)
    idx = jnp.take_along_axis(cand_i, best_j, axis=-1)
    gate = jax.nn.softmax(best_s, axis=-1)
    nb = (B * S) // PEER_BLOCK
    hk = PEER_HEADS * PEER_TOPK
    xs = (x.reshape(nb, PEER_BLOCK, D), idx.reshape(nb, PEER_BLOCK, hk), gate.reshape(nb, PEER_BLOCK, hk))

    def block(args):
        xb, ib, gb = args
        act = jax.nn.gelu(jnp.einsum('tkd,td->tk', u_tab[ib], xb, preferred_element_type=jnp.float32),
                          approximate=False)
        return jnp.einsum('tk,tkd->td', (gb * act).astype(v_tab.dtype), v_tab[ib])

    return lax.map(block, xs).reshape(B, S, D)


def setup_inputs(seed: int = 0) -> dict:
    key = jax.random.key(seed)
    ks = iter(jax.random.split(key, 24))

    def nrm(shape, scale):
        return jax.random.normal(next(ks), shape, jnp.float32) * scale

    L = DEPTH
    return {
        'x': nrm((BATCH, SEQ, D_MODEL), 1.0),
        'mem': nrm((BATCH, MEM_TOKENS, D_MODEL), 1.0),
        'ab_w_in': nrm((N_EVEN, D_MODEL, AB_IN_COLS), D_MODEL ** -0.5),
        'a_rel_bias': nrm((N_EVEN, A_HEADS, 2 * REL_CLIP + 1), 0.5),
        'b_q_norm': 1.0 + nrm((N_EVEN, MLA_Q_LORA), 0.02),
        'b_w_uq': nrm((N_EVEN, MLA_Q_LORA, B_HEADS * (MLA_NOPE + MLA_ROPE)), MLA_Q_LORA ** -0.5),
        'b_kv_norm': 1.0 + nrm((N_EVEN, MLA_KV_LORA), 0.02),
        'b_w_ukv': nrm((N_EVEN, MLA_KV_LORA, B_HEADS * (MLA_NOPE + MLA_V)), MLA_KV_LORA ** -0.5),
        'ab_w_out': nrm((N_EVEN, AB_MIX, D_MODEL), AB_MIX ** -0.5 * DEEPNORM_BETA),
        'cd_w_in': nrm((N_ODD, D_MODEL, CD_IN_COLS), D_MODEL ** -0.5),
        'd_forget_bias': FOX_FORGET_BIAS + nrm((N_ODD, D_HEADS), 0.1),
        'cd_w_out': nrm((N_ODD, CD_MIX, D_MODEL), CD_MIX ** -0.5 * DEEPNORM_BETA),
        'mem_w_q': nrm((L, D_MODEL, MEM_W), D_MODEL ** -0.5),
        'mem_w_kv': nrm((L, D_MODEL, 2 * MEM_W), D_MODEL ** -0.5),
        'mem_w_o': nrm((L, MEM_W, D_MODEL), MEM_W ** -0.5 * DEEPNORM_BETA),
        'peer_w_q': nrm((L, D_MODEL, PEER_HEADS * PEER_DKEY), D_MODEL ** -0.5),
        'peer_sub_keys': nrm((L, PEER_HEADS, 2, N_KEYS, PEER_DKEY // 2), (PEER_DKEY // 2) ** -0.5),
        'peer_u': nrm((L, N_EXPERTS, D_MODEL), D_MODEL ** -0.5),
        'peer_v': nrm((L, N_EXPERTS, D_MODEL), DEEPNORM_BETA * PEER_HEADS ** -0.5),
        'ln_g': 1.0 + nrm((L, 3, D_MODEL), 0.02),
        'ln_b': nrm((L, 3, D_MODEL), 0.02),
    }


def reference(x, mem, ab_w_in, a_rel_bias, b_q_norm, b_w_uq, b_kv_norm, b_w_ukv, ab_w_out,
              cd_w_in, d_forget_bias, cd_w_out, mem_w_q, mem_w_kv, mem_w_o,
              peer_w_q, peer_sub_keys, peer_u, peer_v, ln_g, ln_b):
    h = x
    for layer in range(DEPTH):
        j = layer // 2
        if layer % 2 == 0:
            y = mixer_ab(h, ab_w_in[j], a_rel_bias[j], b_q_norm[j], b_w_uq[j], b_kv_norm[j], b_w_ukv[j], ab_w_out[j])
        else:
            y = mixer_cd(h, cd_w_in[j], d_forget_bias[j], cd_w_out[j])
        h = layer_norm(DEEPNORM_ALPHA * h + y, ln_g[layer, 0], ln_b[layer, 0])
        y = memory_cross_attention(h, mem, mem_w_q[layer], mem_w_kv[layer], mem_w_o[layer])
        h = layer_norm(DEEPNORM_ALPHA * h + y, ln_g[layer, 1], ln_b[layer, 1])
        y = peer_ffn(h, peer_w_q[layer], peer_sub_keys[layer], peer_u[layer], peer_v[layer])
        h = layer_norm(DEEPNORM_ALPHA * h + y, ln_g[layer, 2], ln_b[layer, 2])
    return h
```

```python
import functools

import jax
import jax.numpy as jnp
import numpy as np
from jax import lax
from jax.experimental import pallas as pl
from jax.experimental.pallas import tpu as pltpu

F32 = jnp.float32
BF16 = jnp.bfloat16

CHUNK = 64
HEAD_DIM = 128
A_HEADS = 16
LEFT_CHUNKS = 8
REL_CLIP = 256
B_HEADS = 16
MLA_Q_LORA = 1536
MLA_KV_LORA = 512
MLA_NOPE = 128
MLA_ROPE = 64
MLA_V = 128
C_HEADS = 16
C_KV_HEADS = 4
IDX_HEADS = 32
IDX_HD = 64
DSA_TOPK_MAX = 256
D_HEADS = 16
MEM_HEADS = 4
MEM_HD = 128
PEER_HEADS = 8
PEER_DKEY = 256
N_KEYS = 128
PEER_TOPK = 16
ROPE_THETA = 10000.0
LN_EPS = 1e-5
RMS_EPS = 1e-6
DEPTH = 2
DEEPNORM_ALPHA = (2 * DEPTH) ** 0.25

LANES = 128
V7X_VMEM_LIMIT = 56 * 1024 * 1024
MASK_NEG = -1e30


def _params(sem, vmem=V7X_VMEM_LIMIT):
    return pltpu.CompilerParams(dimension_semantics=sem, vmem_limit_bytes=vmem)


def _round_up(n, m):
    return (n + m - 1) // m * m


def _mm_kernel(a_ref, w_ref, o_ref):
    o_ref[...] = jnp.dot(a_ref[...], w_ref[...], preferred_element_type=F32).astype(o_ref.dtype)


def matmul(a, w, out_dtype, tm=1024, tn=512):
    M, K = a.shape
    N = w.shape[1]
    tm = min(tm, M)
    tn = min(tn, N)
    assert M % tm == 0 and N % tn == 0, (M, N, tm, tn)
    return pl.pallas_call(
        _mm_kernel,
        grid=(M // tm, N // tn),
        in_specs=[pl.BlockSpec((tm, K), lambda i, j: (i, 0)),
                  pl.BlockSpec((K, tn), lambda i, j: (0, j))],
        out_specs=pl.BlockSpec((tm, tn), lambda i, j: (i, j)),
        out_shape=jax.ShapeDtypeStruct((M, N), out_dtype),
        compiler_params=_params(("parallel", "arbitrary")),
        name="matmul",
    )(a, w)


def _ln_kernel(h_ref, y_ref, g_ref, b_ref, o_ref, ob_ref):
    z = DEEPNORM_ALPHA * h_ref[...] + y_ref[...]
    zc = z - jnp.mean(z, axis=-1, keepdims=True)
    var = jnp.mean(zc * zc, axis=-1, keepdims=True)
    out = zc * lax.rsqrt(var + LN_EPS) * g_ref[...] + b_ref[...]
    o_ref[...] = out
    ob_ref[...] = out.astype(BF16)


def deepnorm_ln(h, y, g, b, tb=256):
    T, D = h.shape
    row = pl.BlockSpec((tb, D), lambda i: (i, 0))
    vec = pl.BlockSpec((1, D), lambda i: (0, 0))
    return pl.pallas_call(
        _ln_kernel,
        grid=(T // tb,),
        in_specs=[row, row, vec, vec],
        out_specs=[row, row],
        out_shape=[jax.ShapeDtypeStruct((T, D), F32), jax.ShapeDtypeStruct((T, D), BF16)],
        compiler_params=_params(("parallel",)),
        name="deepnorm_ln",
    )(h, y, g.reshape(1, D), b.reshape(1, D))


def _rms_kernel(x_ref, g_ref, o_ref):
    x = x_ref[...].astype(F32)
    o_ref[...] = (x * lax.rsqrt(jnp.mean(x * x, axis=-1, keepdims=True) + RMS_EPS) * g_ref[...]).astype(o_ref.dtype)


def rms_norm_cols(x, col_block, width, g, tb=512):
    T = x.shape[0]
    return pl.pallas_call(
        _rms_kernel,
        grid=(T // tb,),
        in_specs=[pl.BlockSpec((tb, width), lambda i: (i, col_block)),
                  pl.BlockSpec((1, width), lambda i: (0, 0))],
        out_specs=pl.BlockSpec((tb, width), lambda i: (i, 0)),
        out_shape=jax.ShapeDtypeStruct((T, width), BF16),
        compiler_params=_params(("parallel",)),
        name="rms_norm",
    )(x, g.reshape(1, width))


def rope_tables(seq, dim, group, offset=0, reps=1):
    half = dim // 2
    inv = ROPE_THETA ** (-jnp.arange(half, dtype=F32) * 2.0 / dim)
    ang = jnp.arange(seq, dtype=F32)[:, None] * inv[None, :]
    cos, sin = jnp.cos(ang), jnp.sin(ang)
    zeros_tail = jnp.zeros((seq, group - offset - dim * reps), F32)
    head = jnp.ones((seq, offset), F32)
    zhead = jnp.zeros((seq, offset), F32)
    zhalf = jnp.zeros((seq, half), F32)
    c = jnp.concatenate([head] + [cos, cos] * reps + [zeros_tail], axis=1)
    sa = jnp.concatenate([zhead] + [-sin, zhalf] * reps + [zeros_tail], axis=1)
    sb = jnp.concatenate([zhead] + [zhalf, sin] * reps + [zeros_tail], axis=1)
    return c, sa, sb


def _rope_kernel(x_ref, c_ref, sa_ref, sb_ref, o_ref, *, groups, group, half):
    c, sa, sb = c_ref[...], sa_ref[...], sb_ref[...]
    for g in range(groups):
        x = x_ref[:, g * group:(g + 1) * group].astype(F32)
        out = x * c + pltpu.roll(x, group - half, 1) * sa + pltpu.roll(x, half, 1) * sb
        o_ref[:, g * group:(g + 1) * group] = out.astype(o_ref.dtype)


def rope_cols(x, col_block, groups, group, half, tables, seq, tb=256):
    T = x.shape[0]
    W = groups * group
    nsb = seq // tb
    tab = pl.BlockSpec((tb, group), lambda i: (i % nsb, 0))
    return pl.pallas_call(
        functools.partial(_rope_kernel, groups=groups, group=group, half=half),
        grid=(T // tb,),
        in_specs=[pl.BlockSpec((tb, W), lambda i: (i, col_block)), tab, tab, tab],
        out_specs=pl.BlockSpec((tb, W), lambda i: (i, 0)),
        out_shape=jax.ShapeDtypeStruct((T, W), BF16),
        compiler_params=_params(("parallel",)),
        name="rope",
    )(x, *tables)


def _mla_kv_kernel(kv_ref, kr_ref, c_ref, sa_ref, sb_ref, k_ref, v_ref, *, heads, half):
    kr = kr_ref[...].astype(F32)
    kr = kr * c_ref[...] + pltpu.roll(kr, LANES - half, 1) * sa_ref[...] + pltpu.roll(kr, half, 1) * sb_ref[...]
    kr = kr.astype(k_ref.dtype)
    for h in range(heads):
        k_ref[:, 2 * h * LANES:(2 * h + 1) * LANES] = kv_ref[:, 2 * h * LANES:(2 * h + 1) * LANES]
        k_ref[:, (2 * h + 1) * LANES:(2 * h + 2) * LANES] = kr
        v_ref[:, h * LANES:(h + 1) * LANES] = kv_ref[:, (2 * h + 1) * LANES:(2 * h + 2) * LANES]


def mla_assemble_kv(kv, cproj, kr_col_block, tables, seq, tb=256):
    T, W = kv.shape
    nsb = seq // tb
    tab = pl.BlockSpec((tb, LANES), lambda i: (i % nsb, 0))
    return pl.pallas_call(
        functools.partial(_mla_kv_kernel, heads=B_HEADS, half=MLA_ROPE // 2),
        grid=(T // tb,),
        in_specs=[pl.BlockSpec((tb, W), lambda i: (i, 0)),
                  pl.BlockSpec((tb, LANES), lambda i: (i, kr_col_block)), tab, tab, tab],
        out_specs=[pl.BlockSpec((tb, W), lambda i: (i, 0)), pl.BlockSpec((tb, W // 2), lambda i: (i, 0))],
        out_shape=[jax.ShapeDtypeStruct((T, W), BF16), jax.ShapeDtypeStruct((T, W // 2), BF16)],
        compiler_params=_params(("parallel",)),
        name="mla_assemble_kv",
    )(kv, cproj, *tables)


def _flash_kernel(*refs, hq, hk, dk, dv, qb, kb, scale, mode, fox, q_axis, n_kv_blocks):
    it = iter(refs)
    q_ref, k_ref, v_ref = next(it), next(it), next(it)
    ct_ref = cs_ref = bias_ref = None
    if fox:
        ct_ref, cs_ref = next(it), next(it)
    if mode == "bias":
        bias_ref = next(it)
    o_ref, m_ref, l_ref, acc_ref = next(it), next(it), next(it), next(it)

    qi = pl.program_id(q_axis)
    m_ref[...] = jnp.full(m_ref.shape, MASK_NEG, F32)
    l_ref[...] = jnp.zeros(l_ref.shape, F32)
    acc_ref[...] = jnp.zeros(acc_ref.shape, F32)

    rows = lax.broadcasted_iota(jnp.int32, (qb, kb), 0)
    cols = lax.broadcasted_iota(jnp.int32, (qb, kb), 1)
    if mode == "causal":
        diag_mask = cols <= rows
    elif mode == "chunk":
        diag_mask = (cols // CHUNK) <= (rows // CHUNK)
    else:
        diag_mask = None

    def step(j, masked):
        ks = pl.multiple_of(j * kb, kb)
        for h in range(hq):
            g = h if hk == hq else 0
            q = q_ref[0, :, h * dk:(h + 1) * dk]
            k = k_ref[0, pl.ds(ks, kb), g * dk:(g + 1) * dk]
            v = v_ref[0, pl.ds(ks, kb), g * dv:(g + 1) * dv]
            s = lax.dot_general(q, k, (((1,), (1,)), ((), ())), preferred_element_type=F32) * scale
            if fox:
                s = s + ct_ref[0, h] - cs_ref[0, h, :, pl.ds(ks, kb)]
            if bias_ref is not None:
                s = s + bias_ref[0, :, pl.ds(ks, kb)].astype(F32)
            if masked:
                s = jnp.where(diag_mask, s, MASK_NEG)
            m_prev = m_ref[h]
            m_new = jnp.maximum(m_prev, jnp.max(s, axis=1, keepdims=True))
            alpha = jnp.exp(m_prev - m_new)
            p = jnp.exp(s - m_new)
            l_ref[h] = alpha * l_ref[h] + jnp.sum(p, axis=1, keepdims=True)
            acc_ref[h] = alpha * acc_ref[h] + jnp.dot(p.astype(v.dtype), v, preferred_element_type=F32)
            m_ref[h] = m_new

    def body(j, carry):
        step(j, False)
        return carry

    if mode == "none":
        lax.fori_loop(0, n_kv_blocks, body, 0)
    elif mode == "bias":
        lax.fori_loop(0, qi + 1, body, 0)
    else:
        lax.fori_loop(0, qi, body, 0)
        step(qi, True)

    for h in range(hq):
        o_ref[0, :, h * dv:(h + 1) * dv] = (acc_ref[h] / l_ref[h]).astype(o_ref.dtype)


def flash_attention(q, k, v, *, n_heads, hq, hk, dk, dv, q_blk, k_blk, v_blk, scale, mode,
                    qb=512, kb=512, fox=None, bias=None):
    B, Sq = q.shape[0], q.shape[1]
    Skv = k.shape[1]
    kb = min(kb, Skv)
    nq, ng = Sq // qb, n_heads // hq
    assert hk in (hq, 1)
    if mode in ("causal", "chunk", "bias"):
        assert qb == kb and Sq == Skv
    if mode != "bias":
        grid = (B, ng, nq)
        q_axis = 2

        def spec(shape, fn):
            return pl.BlockSpec(shape, lambda b, g, i: fn(b, g, i))
    else:
        grid = (B, nq, ng)
        q_axis = 1

        def spec(shape, fn):
            return pl.BlockSpec(shape, lambda b, i, g: fn(b, g, i))

    in_specs = [
        spec((1, qb, hq * dk), lambda b, g, i: (b, i, q_blk + g)),
        spec((1, Skv, hk * dk), lambda b, g, i: (b, 0, k_blk + g)),
        spec((1, Skv, hk * dv), lambda b, g, i: (b, 0, v_blk + g)),
    ]
    args = [q, k, v]
    if fox is not None:
        ct, cs = fox
        in_specs += [spec((1, hq, qb, 1), lambda b, g, i: (b, g, i, 0)),
                     spec((1, hq, 1, Skv), lambda b, g, i: (b, g, 0, 0))]
        args += [ct, cs]
    if mode == "bias":
        in_specs.append(spec((1, qb, Skv), lambda b, g, i: (b, i, 0)))
        args.append(bias)
    kern = functools.partial(_flash_kernel, hq=hq, hk=hk, dk=dk, dv=dv, qb=qb, kb=kb, scale=scale, mode=mode,
                             fox=fox is not None, q_axis=q_axis, n_kv_blocks=Skv // kb)
    return pl.pallas_call(
        kern,
        grid=grid,
        in_specs=in_specs,
        out_specs=spec((1, qb, hq * dv), lambda b, g, i: (b, i, g)),
        out_shape=jax.ShapeDtypeStruct((B, Sq, n_heads * dv), BF16),
        scratch_shapes=[pltpu.VMEM((hq, qb, 1), F32), pltpu.VMEM((hq, qb, 1), F32), pltpu.VMEM((hq, qb, dv), F32)],
        compiler_params=_params(("parallel", "parallel", "arbitrary")),
        name="flash_" + mode,
    )(*args)


BAND_QB = 2 * CHUNK
BAND_KBLOCKS = (LEFT_CHUNKS * CHUNK) // BAND_QB + 1
BAND_W = BAND_KBLOCKS * BAND_QB


def band_bias_tiles(rel_bias):
    pad = LEFT_CHUNKS * CHUNK
    i = jnp.arange(BAND_QB)[:, None]
    j = jnp.arange(BAND_W)[None, :]
    rel = i + pad - j
    chunk_diff = i // CHUNK + LEFT_CHUNKS - j // CHUNK
    in_band = (chunk_diff >= 0) & (chunk_diff <= LEFT_CHUNKS)
    bias = rel_bias[:, jnp.clip(rel, -REL_CLIP, REL_CLIP) + REL_CLIP].astype(F32)
    return jnp.where(in_band[None], bias, MASK_NEG)


def _band_kernel(q_ref, k_ref, v_ref, bias_ref, o_ref, *, hb, scale):
    qi = pl.program_id(2)
    for h in range(hb):
        lanes = slice(h * HEAD_DIM, (h + 1) * HEAD_DIM)
        q = q_ref[0, :, lanes]
        s_parts, v_parts = [], []
        for jb in range(BAND_KBLOCKS):
            kblk = qi - (BAND_KBLOCKS - 1) + jb
            ks = pl.multiple_of(jnp.maximum(kblk, 0) * BAND_QB, BAND_QB)
            k = k_ref[0, pl.ds(ks, BAND_QB), lanes]
            v_parts.append(v_ref[0, pl.ds(ks, BAND_QB), lanes])
            s = lax.dot_general(q, k, (((1,), (1,)), ((), ())), preferred_element_type=F32) * scale
            s = s + bias_ref[h, :, jb * BAND_QB:(jb + 1) * BAND_QB]
            s_parts.append(jnp.where(kblk >= 0, s, MASK_NEG))
        s = jnp.concatenate(s_parts, axis=1)
        p = jnp.exp(s - jnp.max(s, axis=1, keepdims=True))
        l = jnp.sum(p, axis=1, keepdims=True)
        acc = jnp.zeros((BAND_QB, HEAD_DIM), F32)
        for jb in range(BAND_KBLOCKS):
            pj = p[:, jb * BAND_QB:(jb + 1) * BAND_QB].astype(v_parts[jb].dtype)
            acc = acc + jnp.dot(pj, v_parts[jb], preferred_element_type=F32)
        o_ref[0, :, lanes] = (acc / l).astype(o_ref.dtype)


def band_attention(qkv, bias_tiles, hb=4):
    B, S, _ = qkv.shape
    H = A_HEADS
    ng = H // hb
    return pl.pallas_call(
        functools.partial(_band_kernel, hb=hb, scale=HEAD_DIM ** -0.5),
        grid=(B, ng, S // BAND_QB),
        in_specs=[pl.BlockSpec((1, BAND_QB, hb * HEAD_DIM), lambda b, g, i: (b, i, g)),
                  pl.BlockSpec((1, S, hb * HEAD_DIM), lambda b, g, i: (b, 0, ng + g)),
                  pl.BlockSpec((1, S, hb * HEAD_DIM), lambda b, g, i: (b, 0, 2 * ng + g)),
                  pl.BlockSpec((hb, BAND_QB, BAND_W), lambda b, g, i: (g, 0, 0))],
        out_specs=pl.BlockSpec((1, BAND_QB, hb * HEAD_DIM), lambda b, g, i: (b, i, g)),
        out_shape=jax.ShapeDtypeStruct((B, S, H * HEAD_DIM), BF16),
        compiler_params=_params(("parallel", "parallel", "arbitrary")),
        name="band_attention",
    )(qkv, qkv, qkv, bias_tiles)


def _fox_cumsum_kernel(f_ref, b_ref, o_ref, carry_ref, *, cb):
    @pl.when(pl.program_id(1) == 0)
    def _():
        carry_ref[...] = jnp.zeros(carry_ref.shape, F32)

    x = f_ref[0].astype(F32) + b_ref[...]
    log_f = jnp.minimum(x, 0.0) - jnp.log1p(jnp.exp(-jnp.abs(x)))
    r = lax.broadcasted_iota(jnp.int32, (cb, cb), 0)
    c = lax.broadcasted_iota(jnp.int32, (cb, cb), 1)
    tri = jnp.where(c <= r, 1.0, 0.0).astype(F32)
    cum = jnp.dot(tri, log_f, preferred_element_type=F32, precision=lax.Precision.HIGHEST) + carry_ref[...]
    o_ref[0] = cum
    carry_ref[...] = cum[cb - 1:cb, :]


def fox_cumsum(proj, col_block, bias_row, batch, seq, cb=256):
    x = proj.reshape(batch, seq, proj.shape[-1])
    return pl.pallas_call(
        functools.partial(_fox_cumsum_kernel, cb=cb),
        grid=(batch, seq // cb),
        in_specs=[pl.BlockSpec((1, cb, LANES), lambda b, i: (b, i, col_block)),
                  pl.BlockSpec((1, LANES), lambda b, i: (0, 0))],
        out_specs=pl.BlockSpec((1, cb, LANES), lambda b, i: (b, i, 0)),
        out_shape=jax.ShapeDtypeStruct((batch, seq, LANES), F32),
        scratch_shapes=[pltpu.VMEM((1, LANES), F32)],
        compiler_params=_params(("parallel", "arbitrary")),
        name="fox_cumsum",
    )(x, bias_row)


IDX_QB = 128
IDX_KC = 512


def _sortable_key(x):
    bits = pltpu.bitcast(x, jnp.int32)
    return bits ^ ((bits >> 31) & jnp.int32(0x7FFFFFFF))


def _indexer_kernel(ki_ref, qit_ref, w_ref, o_ref, key_ref, *, seq, topk):
    qi = pl.program_id(1)
    q0 = qi * IDX_QB
    n_chunks = (q0 + IDX_QB + IDX_KC - 1) // IDX_KC
    qchunk = (q0 + lax.broadcasted_iota(jnp.int32, (1, IDX_QB), 1)) // CHUNK
    int_min = jnp.int32(-2 ** 31)
    w_scale = IDX_HEADS ** -0.5 * IDX_HD ** -0.5

    def score_chunk(c, carry):
        ks = pl.multiple_of(c * IDX_KC, IDX_KC)
        ki = ki_ref[0, pl.ds(ks, IDX_KC), :][:, :IDX_HD]
        acc = jnp.zeros((IDX_KC, IDX_QB), F32)
        for hp in range(IDX_HEADS // 2):
            t = jnp.dot(ki, qit_ref[0, 0, :, hp * 2 * IDX_QB:(hp + 1) * 2 * IDX_QB], preferred_element_type=F32)
            w = w_ref[0, 0, :, hp * 2 * IDX_QB:(hp + 1) * 2 * IDX_QB] * w_scale
            t = jnp.maximum(t, 0.0) * w
            acc = acc + t[:, :IDX_QB] + t[:, IDX_QB:]
        kchunk = (ks + lax.broadcasted_iota(jnp.int32, (IDX_KC, 1), 0)) // CHUNK
        acc = jnp.where(kchunk <= qchunk, acc, -jnp.inf)
        key_ref[pl.ds(ks, IDX_KC), :] = _sortable_key(acc)
        return carry

    lax.fori_loop(0, n_chunks, score_chunk, 0)

    def count_ge(cand):
        def body(c, acc):
            ks = pl.multiple_of(c * IDX_KC, IDX_KC)
            blk = key_ref[pl.ds(ks, IDX_KC), :]
            hit = jnp.where(blk >= cand, 1, 0).astype(jnp.int32)
            return acc + jnp.sum(hit.reshape(IDX_KC // 8, 8, IDX_QB), axis=0)
        acc = lax.fori_loop(0, n_chunks, body, jnp.zeros((8, IDX_QB), jnp.int32))
        return jnp.sum(acc, axis=0, keepdims=True)

    def bit_step(i, ans):
        bit = lax.shift_left(jnp.int32(1), 31 - i)
        cand = ans | bit
        cnt = count_ge(cand ^ int_min)
        return jnp.where(cnt >= topk, cand, ans)

    ans = lax.fori_loop(0, 32, bit_step, jnp.zeros((1, IDX_QB), jnp.int32))
    thr = ans ^ int_min
    neg_inf_key = _sortable_key(jnp.full((1, IDX_QB), -jnp.inf, F32))

    def emit(c, carry):
        ks = pl.multiple_of(c * IDX_KC, IDX_KC)
        blk = key_ref[pl.ds(ks, IDX_KC), :]
        sel = (blk >= thr) & (blk > neg_inf_key)
        bias_t = jnp.where(sel, 0.0, MASK_NEG).astype(F32)
        o_ref[0, :, pl.ds(ks, IDX_KC)] = bias_t.T.astype(o_ref.dtype)
        return carry

    lax.fori_loop(0, n_chunks, emit, 0)

    def fill(c, carry):
        ks = pl.multiple_of(c * IDX_KC, IDX_KC)
        o_ref[0, :, pl.ds(ks, IDX_KC)] = jnp.full((IDX_QB, IDX_KC), MASK_NEG, o_ref.dtype)
        return carry

    lax.fori_loop(n_chunks, seq // IDX_KC, fill, 0)


def dsa_selection_bias(ki, qit, wt, topk):
    B, S, _ = ki.shape
    nq = S // IDX_QB
    return pl.pallas_call(
        functools.partial(_indexer_kernel, seq=S, topk=topk),
        grid=(B, nq),
        in_specs=[pl.BlockSpec((1, S, LANES), lambda b, i: (b, 0, 0)),
                  pl.BlockSpec((1, 1, IDX_HD, IDX_HEADS * IDX_QB), lambda b, i: (b, i, 0, 0)),
                  pl.BlockSpec((1, 1, 1, IDX_HEADS * IDX_QB), lambda b, i: (b, i, 0, 0))],
        out_specs=pl.BlockSpec((1, IDX_QB, S), lambda b, i: (b, i, 0)),
        out_shape=jax.ShapeDtypeStruct((B, S, S), BF16),
        scratch_shapes=[pltpu.VMEM((S, IDX_QB), jnp.int32)],
        compiler_params=_params(("parallel", "arbitrary")),
        name="dsa_indexer",
    )(ki, qit, wt)


PEER_TB_ROUTE = 128
STAT_ROWS = 8


def _peer_route_kernel(qt_ref, sk_ref, s_ref, st_ref, top_ref):
    nsub = 2 * PEER_HEADS
    half = PEER_DKEY // 2
    tb = PEER_TB_ROUTE

    def sub_scores(hp, carry):
        r0 = pl.multiple_of(hp * half, half)
        q = qt_ref[pl.ds(r0, half), :].astype(BF16)
        s = jnp.dot(sk_ref[hp], q, preferred_element_type=F32)
        s_ref[hp] = s

        def extract(i, x):
            m = jnp.max(x, axis=0, keepdims=True)
            top_ref[hp, pl.ds(i, 1), :] = m
            return jnp.where(x == m, -jnp.inf, x)

        lax.fori_loop(0, PEER_TOPK, extract, s)
        return carry

    lax.fori_loop(0, nsub, sub_scores, 0)

    def head_stats(h, carry):
        a = top_ref[2 * h]
        b = top_ref[2 * h + 1]
        cand = jnp.concatenate([a[i:i + 1, :] + b for i in range(PEER_TOPK)], axis=0)

        def extract(i, c):
            x, mx, z, _ = c
            m = jnp.max(x, axis=0, keepdims=True)
            mx = jnp.where(i == 0, m, mx)
            z = z + jnp.exp(m - mx)
            return jnp.where(x == m, -jnp.inf, x), mx, z, m

        zero = jnp.zeros((1, tb), F32)
        _, mx, z, last = lax.fori_loop(0, PEER_TOPK, extract, (cand, zero, zero, zero))
        st_ref[h, 0:1, :] = last
        st_ref[h, 1:2, :] = a[0:1, :]
        st_ref[h, 2:3, :] = b[0:1, :]
        st_ref[h, 3:4, :] = 1.0 / z
        st_ref[h, 4:STAT_ROWS, :] = jnp.zeros((STAT_ROWS - 4, tb), F32)
        return carry

    lax.fori_loop(0, PEER_HEADS, head_stats, 0)


def peer_route(qt, sub_keys):
    R, T = qt.shape
    tb = PEER_TB_ROUTE
    nsub = 2 * PEER_HEADS
    return pl.pallas_call(
        _peer_route_kernel,
        grid=(T // tb,),
        in_specs=[pl.BlockSpec((R, tb), lambda i: (0, i)),
                  pl.BlockSpec((nsub, N_KEYS, PEER_DKEY // 2), lambda i: (0, 0, 0))],
        out_specs=[pl.BlockSpec((nsub, N_KEYS, tb), lambda i: (0, 0, i)),
                   pl.BlockSpec((PEER_HEADS, STAT_ROWS, tb), lambda i: (0, 0, i))],
        out_shape=[jax.ShapeDtypeStruct((nsub, N_KEYS, T), F32),
                   jax.ShapeDtypeStruct((PEER_HEADS, STAT_ROWS, T), F32)],
        scratch_shapes=[pltpu.VMEM((nsub, PEER_TOPK, tb), F32)],
        compiler_params=_params(("parallel",)),
        name="peer_route",
    )(qt, sub_keys)


PEER_TB = 512
PEER_EB = 512


def _gelu_exact(x):
    return 0.5 * x * (1.0 + lax.erf(x * (2.0 ** -0.5)))


def _peer_dense_kernel(xt_ref, u_ref, v_ref, s_ref, st_ref, y_ref, e2_ref, ht_ref):
    e = pl.program_id(1)

    @pl.when(e == 0)
    def _():
        y_ref[...] = jnp.zeros(y_ref.shape, F32)
        for h in range(PEER_HEADS):
            e2_ref[h] = jnp.exp(s_ref[2 * h + 1] - st_ref[h, 2:3, :])

    act = _gelu_exact(jnp.dot(u_ref[...], xt_ref[...], preferred_element_type=F32))
    rows_per_step = PEER_EB // N_KEYS
    for r in range(rows_per_step):
        i1 = e * rows_per_step + r
        gate = jnp.zeros((N_KEYS, PEER_TB), F32)
        for h in range(PEER_HEADS):
            s1 = s_ref[2 * h, pl.ds(i1, 1), :]
            e1 = jnp.exp(s1 - st_ref[h, 1:2, :]) * st_ref[h, 3:4, :]
            pair = s_ref[2 * h + 1] + s1
            gate = gate + jnp.where(pair >= st_ref[h, 0:1, :], e2_ref[h] * e1, 0.0)
        ht_ref[r * N_KEYS:(r + 1) * N_KEYS, :] = (gate * act[r * N_KEYS:(r + 1) * N_KEYS, :]).astype(BF16)
    y_ref[...] += lax.dot_general(ht_ref[...], v_ref[...], (((0,), (0,)), ((), ())), preferred_element_type=F32)


def peer_dense(xt, u, v, s_t, stats):
    D, T = xt.shape
    E = u.shape[0]
    nsub = 2 * PEER_HEADS
    once = pl.Buffered(1)
    return pl.pallas_call(
        _peer_dense_kernel,
        grid=(T // PEER_TB, E // PEER_EB),
        in_specs=[pl.BlockSpec((D, PEER_TB), lambda i, e: (0, i), pipeline_mode=once),
                  pl.BlockSpec((PEER_EB, D), lambda i, e: (e, 0)),
                  pl.BlockSpec((PEER_EB, D), lambda i, e: (e, 0)),
                  pl.BlockSpec((nsub, N_KEYS, PEER_TB), lambda i, e: (0, 0, i), pipeline_mode=once),
                  pl.BlockSpec((PEER_HEADS, STAT_ROWS, PEER_TB), lambda i, e: (0, 0, i))],
        out_specs=pl.BlockSpec((PEER_TB, D), lambda i, e: (i, 0)),
        out_shape=jax.ShapeDtypeStruct((T, D), F32),
        scratch_shapes=[pltpu.VMEM((PEER_HEADS, N_KEYS, PEER_TB), F32), pltpu.VMEM((PEER_EB, PEER_TB), BF16)],
        compiler_params=_params(("parallel", "arbitrary")),
        name="peer_dense",
    )(xt, u, v, s_t, stats)


def _pad_cols(w, width):
    return jnp.pad(w, ((0, 0), (0, width - w.shape[1])))


def mixer_ab(hb, batch, seq, w_in, rel_bias, q_norm, w_uq, kv_norm, w_ukv, w_out):
    T, D = hb.shape
    a_w = A_HEADS * HEAD_DIM
    w_qkv = w_in[:, :3 * a_w].astype(BF16)
    c_cols = MLA_Q_LORA + MLA_KV_LORA + LANES
    w_c = _pad_cols(w_in[:, 3 * a_w:], _round_up(c_cols, 512)).astype(BF16)
    qkv = matmul(hb, w_qkv, BF16)
    cproj = matmul(hb, w_c, F32)

    oa = band_attention(qkv.reshape(batch, seq, 3 * a_w), band_bias_tiles(rel_bias))

    qh = MLA_NOPE + MLA_ROPE
    w_uq_p = jnp.pad(w_uq.reshape(MLA_Q_LORA, B_HEADS, qh), ((0, 0), (0, 0), (0, 2 * LANES - qh)))
    w_uq_p = w_uq_p.reshape(MLA_Q_LORA, B_HEADS * 2 * LANES).astype(BF16)
    cq_n = rms_norm_cols(cproj, 0, MLA_Q_LORA, q_norm)
    ckv_n = rms_norm_cols(cproj, MLA_Q_LORA // MLA_KV_LORA, MLA_KV_LORA, kv_norm)
    q_lat = matmul(cq_n, w_uq_p, F32)
    q_cat = rope_cols(q_lat, 0, B_HEADS, 2 * LANES, MLA_ROPE // 2,
                      rope_tables(seq, MLA_ROPE, 2 * LANES, offset=MLA_NOPE), seq)
    kv = matmul(ckv_n, w_ukv.astype(BF16), BF16)
    k_cat, v_cat = mla_assemble_kv(kv, cproj, (MLA_Q_LORA + MLA_KV_LORA) // LANES,
                                   rope_tables(seq, MLA_ROPE, LANES), seq)
    ob = flash_attention(q_cat.reshape(batch, seq, -1), k_cat.reshape(batch, seq, -1), v_cat.reshape(batch, seq, -1),
                         n_heads=B_HEADS, hq=2, hk=2, dk=2 * LANES, dv=MLA_V, q_blk=0, k_blk=0, v_blk=0,
                         scale=(MLA_NOPE + MLA_ROPE) ** -0.5, mode="chunk")
    o = jnp.concatenate([oa.reshape(T, -1), ob.reshape(T, -1)], axis=-1)
    return matmul(o, w_out.astype(BF16), F32)


def mixer_cd(hb, batch, seq, w_in, forget_bias, w_out):
    T, D = hb.shape
    c_qw, c_kw, d_w = C_HEADS * HEAD_DIM, C_KV_HEADS * HEAD_DIM, D_HEADS * HEAD_DIM
    i_w = IDX_HEADS * IDX_HD
    offs = np.cumsum([0, c_qw, c_kw, c_kw, i_w, IDX_HD, IDX_HEADS, d_w, d_w, d_w, D_HEADS])
    col = lambda n: w_in[:, offs[n]:offs[n + 1]]
    w_a = jnp.concatenate([col(0), col(1), col(2), col(6), col(7), col(8)], axis=1).astype(BF16)
    w_b = jnp.concatenate([col(3), _pad_cols(col(4), LANES), _pad_cols(col(5), LANES), _pad_cols(col(9), LANES)], axis=1)
    w_b = _pad_cols(w_b, _round_up(w_b.shape[1], 512)).astype(BF16)
    proj_a = matmul(hb, w_a, BF16)
    proj_b = matmul(hb, w_b, F32)
    qk_w = c_qw + c_kw
    blk_ki, blk_wi, blk_fd = i_w // LANES, i_w // LANES + 1, i_w // LANES + 2

    qk_rot = rope_cols(proj_a, 0, qk_w // LANES, LANES, HEAD_DIM // 2, rope_tables(seq, HEAD_DIM, LANES), seq)
    qi_rot = rope_cols(proj_b, 0, i_w // LANES, LANES, IDX_HD // 2, rope_tables(seq, IDX_HD, LANES, reps=2), seq)
    ki_rot = rope_cols(proj_b, blk_ki, 1, LANES, IDX_HD // 2, rope_tables(seq, IDX_HD, LANES), seq)
    nq = seq // IDX_QB
    qit = qi_rot.reshape(batch, nq, IDX_QB, IDX_HEADS, IDX_HD).transpose(0, 1, 4, 3, 2)
    qit = qit.reshape(batch, nq, IDX_HD, IDX_HEADS * IDX_QB)
    wi = proj_b[:, blk_wi * LANES:blk_wi * LANES + IDX_HEADS]
    wt = wi.reshape(batch, nq, IDX_QB, IDX_HEADS).transpose(0, 1, 3, 2).reshape(batch, nq, 1, IDX_HEADS * IDX_QB)
    sel_bias = dsa_selection_bias(ki_rot.reshape(batch, seq, LANES), qit, wt, min(DSA_TOPK_MAX, seq // 4))
    qk3 = qk_rot.reshape(batch, seq, qk_w)
    pa3 = proj_a.reshape(batch, seq, -1)
    rep = C_HEADS // C_KV_HEADS
    oc = flash_attention(qk3, qk3, pa3, n_heads=C_HEADS, hq=rep, hk=1, dk=HEAD_DIM, dv=HEAD_DIM,
                         q_blk=0, k_blk=c_qw // HEAD_DIM, v_blk=qk_w // HEAD_DIM,
                         scale=HEAD_DIM ** -0.5, mode="bias", bias=sel_bias)

    fbias = _pad_cols(forget_bias.reshape(1, D_HEADS).astype(F32), LANES)
    cum = fox_cumsum(proj_b, blk_fd, fbias, batch, seq)[:, :, :D_HEADS]
    cum_t = cum.transpose(0, 2, 1)
    hd = 2
    d0 = (qk_w + c_kw) // (hd * HEAD_DIM)
    od = flash_attention(pa3, pa3, pa3, n_heads=D_HEADS, hq=hd, hk=hd, dk=HEAD_DIM, dv=HEAD_DIM,
                         q_blk=d0, k_blk=d0 + d_w // (hd * HEAD_DIM), v_blk=d0 + 2 * d_w // (hd * HEAD_DIM),
                         scale=HEAD_DIM ** -0.5, mode="causal", fox=(cum_t[..., None], cum_t[:, :, None, :]))
    o = jnp.concatenate([oc.reshape(T, -1), od.reshape(T, -1)], axis=-1)
    return matmul(o, w_out.astype(BF16), F32)


def memory_cross_attention(hb, mem_b, batch, seq, w_q, w_kv, w_o):
    T, D = hb.shape
    mem_w = MEM_HEADS * MEM_HD
    q = matmul(hb, w_q.astype(BF16), BF16)
    kv = matmul(mem_b.reshape(-1, D), w_kv.astype(BF16), BF16)
    kv3 = kv.reshape(batch, -1, 2 * mem_w)
    o = flash_attention(q.reshape(batch, seq, mem_w), kv3, kv3, n_heads=MEM_HEADS, hq=MEM_HEADS, hk=MEM_HEADS,
                        dk=MEM_HD, dv=MEM_HD, q_blk=0, k_blk=0, v_blk=1, scale=MEM_HD ** -0.5, mode="none",
                        kb=kv3.shape[1])
    return matmul(o.reshape(T, mem_w), w_o.astype(BF16), F32)


def peer_ffn(hb, w_q, sub_keys, u_tab, v_tab):
    ht = hb.T
    qt = matmul(w_q.T.astype(BF16), ht, F32)
    sk = sub_keys.reshape(2 * PEER_HEADS, N_KEYS, PEER_DKEY // 2).astype(BF16)
    s_t, stats = peer_route(qt, sk)
    return peer_dense(ht, u_tab.astype(BF16), v_tab.astype(BF16), s_t, stats)


def kernel(x, mem, ab_w_in, a_rel_bias, b_q_norm, b_w_uq, b_kv_norm, b_w_ukv, ab_w_out, cd_w_in, d_forget_bias,
           cd_w_out, mem_w_q, mem_w_kv, mem_w_o, peer_w_q, peer_sub_keys, peer_u, peer_v, ln_g, ln_b):
    batch, seq, d_model = x.shape
    h = x.reshape(batch * seq, d_model)
    hb = h.astype(BF16)
    mem_b = mem.astype(BF16)
    for layer in range(DEPTH):
        j = layer // 2
        if layer % 2 == 0:
            y = mixer_ab(hb, batch, seq, ab_w_in[j], a_rel_bias[j], b_q_norm[j], b_w_uq[j], b_kv_norm[j],
                         b_w_ukv[j], ab_w_out[j])
        else:
            y = mixer_cd(hb, batch, seq, cd_w_in[j], d_forget_bias[j], cd_w_out[j])
        h, hb = deepnorm_ln(h, y, ln_g[layer, 0], ln_b[layer, 0])
        y = memory_cross_attention(hb, mem_b, batch, seq, mem_w_q[layer], mem_w_kv[layer], mem_w_o[layer])
        h, hb = deepnorm_ln(h, y, ln_g[layer, 1], ln_b[layer, 1])
        y = peer_ffn(hb, peer_w_q[layer], peer_sub_keys[layer], peer_u[layer], peer_v[layer])
        h, hb = deepnorm_ln(h, y, ln_g[layer, 2], ln_b[layer, 2])
    return h.reshape(batch, seq, d_model)
```

```python
import functools

import jax
import jax.numpy as jnp
import numpy as np
from jax import lax
from jax.experimental import pallas as pl
from jax.experimental.pallas import tpu as pltpu

F32 = jnp.float32
BF16 = jnp.bfloat16

CHUNK = 64
HEAD_DIM = 128
A_HEADS = 16
LEFT_CHUNKS = 8
REL_CLIP = 256
B_HEADS = 16
MLA_Q_LORA = 1536
MLA_KV_LORA = 512
MLA_NOPE = 128
MLA_ROPE = 64
MLA_V = 128
C_HEADS = 16
C_KV_HEADS = 4
IDX_HEADS = 32
IDX_HD = 64
DSA_TOPK_MAX = 256
D_HEADS = 16
MEM_HEADS = 4
MEM_HD = 128
PEER_HEADS = 8
PEER_DKEY = 256
N_KEYS = 128
PEER_TOPK = 16
ROPE_THETA = 10000.0
LN_EPS = 1e-5
RMS_EPS = 1e-6
DEPTH = 2
DEEPNORM_ALPHA = (2 * DEPTH) ** 0.25

LANES = 128
V7X_VMEM_LIMIT = 56 * 1024 * 1024
MASK_NEG = -1e30


def _params(sem, vmem=V7X_VMEM_LIMIT, flags=None):
    return pltpu.CompilerParams(dimension_semantics=sem, vmem_limit_bytes=vmem, flags=flags)


def _round_up(n, m):
    return (n + m - 1) // m * m


def _mm_kernel(a_ref, w_ref, o_ref):
    o_ref[...] = jnp.dot(a_ref[...], w_ref[...], preferred_element_type=F32).astype(o_ref.dtype)


def matmul(a, w, out_dtype, tm=1024, tn=512):
    M, K = a.shape
    N = w.shape[1]
    tm = min(tm, M)
    tn = min(tn, N)
    assert M % tm == 0 and N % tn == 0, (M, N, tm, tn)
    return pl.pallas_call(
        _mm_kernel,
        grid=(M // tm, N // tn),
        in_specs=[pl.BlockSpec((tm, K), lambda i, j: (i, 0)),
                  pl.BlockSpec((K, tn), lambda i, j: (0, j))],
        out_specs=pl.BlockSpec((tm, tn), lambda i, j: (i, j)),
        out_shape=jax.ShapeDtypeStruct((M, N), out_dtype),
        compiler_params=_params(("parallel", "arbitrary")),
        name="matmul",
    )(a, w)


def _ln_kernel(h_ref, y_ref, g_ref, b_ref, o_ref, ob_ref):
    z = DEEPNORM_ALPHA * h_ref[...] + y_ref[...]
    zc = z - jnp.mean(z, axis=-1, keepdims=True)
    var = jnp.mean(zc * zc, axis=-1, keepdims=True)
    out = zc * lax.rsqrt(var + LN_EPS) * g_ref[...] + b_ref[...]
    o_ref[...] = out
    ob_ref[...] = out.astype(BF16)


def deepnorm_ln(h, y, g, b, tb=256):
    T, D = h.shape
    row = pl.BlockSpec((tb, D), lambda i: (i, 0))
    vec = pl.BlockSpec((1, D), lambda i: (0, 0))
    return pl.pallas_call(
        _ln_kernel,
        grid=(T // tb,),
        in_specs=[row, row, vec, vec],
        out_specs=[row, row],
        out_shape=[jax.ShapeDtypeStruct((T, D), F32), jax.ShapeDtypeStruct((T, D), BF16)],
        compiler_params=_params(("parallel",)),
        name="deepnorm_ln",
    )(h, y, g.reshape(1, D), b.reshape(1, D))


def _rms_kernel(x_ref, g_ref, o_ref):
    x = x_ref[...].astype(F32)
    o_ref[...] = (x * lax.rsqrt(jnp.mean(x * x, axis=-1, keepdims=True) + RMS_EPS) * g_ref[...]).astype(o_ref.dtype)


def rms_norm_cols(x, col_block, width, g, tb=512):
    T = x.shape[0]
    return pl.pallas_call(
        _rms_kernel,
        grid=(T // tb,),
        in_specs=[pl.BlockSpec((tb, width), lambda i: (i, col_block)),
                  pl.BlockSpec((1, width), lambda i: (0, 0))],
        out_specs=pl.BlockSpec((tb, width), lambda i: (i, 0)),
        out_shape=jax.ShapeDtypeStruct((T, width), BF16),
        compiler_params=_params(("parallel",)),
        name="rms_norm",
    )(x, g.reshape(1, width))


def rope_tables(seq, dim, group, offset=0, reps=1):
    half = dim // 2
    inv = ROPE_THETA ** (-jnp.arange(half, dtype=F32) * 2.0 / dim)
    ang = jnp.arange(seq, dtype=F32)[:, None] * inv[None, :]
    cos, sin = jnp.cos(ang), jnp.sin(ang)
    zeros_tail = jnp.zeros((seq, group - offset - dim * reps), F32)
    head = jnp.ones((seq, offset), F32)
    zhead = jnp.zeros((seq, offset), F32)
    zhalf = jnp.zeros((seq, half), F32)
    c = jnp.concatenate([head] + [cos, cos] * reps + [zeros_tail], axis=1)
    sa = jnp.concatenate([zhead] + [-sin, zhalf] * reps + [zeros_tail], axis=1)
    sb = jnp.concatenate([zhead] + [zhalf, sin] * reps + [zeros_tail], axis=1)
    return c, sa, sb


def _rope_kernel(x_ref, c_ref, sa_ref, sb_ref, o_ref, *, groups, group, half):
    c, sa, sb = c_ref[...], sa_ref[...], sb_ref[...]
    for g in range(groups):
        x = x_ref[:, g * group:(g + 1) * group].astype(F32)
        out = x * c + pltpu.roll(x, group - half, 1) * sa + pltpu.roll(x, half, 1) * sb
        o_ref[:, g * group:(g + 1) * group] = out.astype(o_ref.dtype)


def rope_cols(x, col_block, groups, group, half, tables, seq, tb=256):
    T = x.shape[0]
    W = groups * group
    nsb = seq // tb
    tab = pl.BlockSpec((tb, group), lambda i: (i % nsb, 0))
    return pl.pallas_call(
        functools.partial(_rope_kernel, groups=groups, group=group, half=half),
        grid=(T // tb,),
        in_specs=[pl.BlockSpec((tb, W), lambda i: (i, col_block)), tab, tab, tab],
        out_specs=pl.BlockSpec((tb, W), lambda i: (i, 0)),
        out_shape=jax.ShapeDtypeStruct((T, W), BF16),
        compiler_params=_params(("parallel",)),
        name="rope",
    )(x, *tables)


def _mla_kv_kernel(kv_ref, kr_ref, c_ref, sa_ref, sb_ref, k_ref, v_ref, *, heads, half):
    kr = kr_ref[...].astype(F32)
    kr = kr * c_ref[...] + pltpu.roll(kr, LANES - half, 1) * sa_ref[...] + pltpu.roll(kr, half, 1) * sb_ref[...]
    kr = kr.astype(k_ref.dtype)
    for h in range(heads):
        k_ref[:, 2 * h * LANES:(2 * h + 1) * LANES] = kv_ref[:, 2 * h * LANES:(2 * h + 1) * LANES]
        k_ref[:, (2 * h + 1) * LANES:(2 * h + 2) * LANES] = kr
        v_ref[:, h * LANES:(h + 1) * LANES] = kv_ref[:, (2 * h + 1) * LANES:(2 * h + 2) * LANES]


def mla_assemble_kv(kv, cproj, kr_col_block, tables, seq, tb=256):
    T, W = kv.shape
    nsb = seq // tb
    tab = pl.BlockSpec((tb, LANES), lambda i: (i % nsb, 0))
    return pl.pallas_call(
        functools.partial(_mla_kv_kernel, heads=B_HEADS, half=MLA_ROPE // 2),
        grid=(T // tb,),
        in_specs=[pl.BlockSpec((tb, W), lambda i: (i, 0)),
                  pl.BlockSpec((tb, LANES), lambda i: (i, kr_col_block)), tab, tab, tab],
        out_specs=[pl.BlockSpec((tb, W), lambda i: (i, 0)), pl.BlockSpec((tb, W // 2), lambda i: (i, 0))],
        out_shape=[jax.ShapeDtypeStruct((T, W), BF16), jax.ShapeDtypeStruct((T, W // 2), BF16)],
        compiler_params=_params(("parallel",)),
        name="mla_assemble_kv",
    )(kv, cproj, *tables)


FLASH_RB = 128


def _flash_kernel(*refs, hq, hk, dk, dv, qb, kb, scale, mode, fox, q_axis, n_kv_blocks):
    it = iter(refs)
    q_ref, k_ref, v_ref = next(it), next(it), next(it)
    ct_ref = cs_ref = bias_ref = None
    if fox:
        ct_ref, cs_ref = next(it), next(it)
    if mode == "bias":
        bias_ref = next(it)
    o_ref, m_ref, l_ref, acc_ref, p_ref, alpha_ref = next(it), next(it), next(it), next(it), next(it), next(it)
    ct_lanes_ref = next(it) if fox else None

    qi = pl.program_id(q_axis)
    m_ref[...] = jnp.full(m_ref.shape, MASK_NEG, F32)
    l_ref[...] = jnp.zeros(l_ref.shape, F32)
    acc_ref[...] = jnp.zeros(acc_ref.shape, F32)
    if fox:
        for h in range(hq):
            ct_lanes_ref[h] = jnp.broadcast_to(ct_ref[0, h], (qb, LANES))

    n_sub = qb // FLASH_RB
    per_kv = kb // FLASH_RB
    rows = lax.broadcasted_iota(jnp.int32, (FLASH_RB, kb), 0)
    cols = lax.broadcasted_iota(jnp.int32, (FLASH_RB, kb), 1)

    def diag_mask(r):
        row_in_kv = rows + (r % per_kv) * FLASH_RB
        if mode == "causal":
            return cols <= row_in_kv
        if mode == "chunk":
            return (cols // CHUNK) <= (row_in_kv // CHUNK)
        return None

    def step(j, visible):
        ks = pl.multiple_of(j * kb, kb)
        for h in range(hq):
            g = h if hk == hq else 0
            k = k_ref[0, pl.ds(ks, kb), g * dk:(g + 1) * dk]
            v = v_ref[0, pl.ds(ks, kb), g * dv:(g + 1) * dv]
            s_all = lax.dot_general(q_ref[0, :, h * dk:(h + 1) * dk], k, (((1,), (1,)), ((), ())),
                                    preferred_element_type=F32)
            cs = cs_ref[0, h, :, pl.ds(ks, kb)] if fox else None
            for r in range(n_sub):
                rs = slice(r * FLASH_RB, (r + 1) * FLASH_RB)
                if visible[r] == "none":
                    p_ref[h, rs, :] = jnp.zeros((FLASH_RB, kb), p_ref.dtype)
                    alpha_ref[h, rs, :] = jnp.ones((FLASH_RB, LANES), F32)
                    continue
                s = s_all[rs] * scale
                if bias_ref is not None:
                    s = s + bias_ref[0, rs, pl.ds(ks, kb)].astype(F32)
                blocks = [s[:, c * LANES:(c + 1) * LANES] for c in range(kb // LANES)]
                if fox:
                    ct = ct_lanes_ref[h, rs]
                    blocks = [blk + ct - cs[:, c * LANES:(c + 1) * LANES] for c, blk in enumerate(blocks)]
                if visible[r] == "diag" and mode in ("causal", "chunk"):
                    mask = diag_mask(r)
                    blocks = [jnp.where(mask[:, c * LANES:(c + 1) * LANES], blk, MASK_NEG)
                              for c, blk in enumerate(blocks)]
                blk_max = blocks[0]
                for blk in blocks[1:]:
                    blk_max = jnp.maximum(blk_max, blk)
                m_prev = m_ref[h, rs]
                m_new = jnp.maximum(m_prev, jnp.max(blk_max, axis=1, keepdims=True))
                alpha = jnp.exp(m_prev - m_new)
                p_blocks = [jnp.exp(blk - m_new) for blk in blocks]
                p_lanes = p_blocks[0]
                for pb in p_blocks[1:]:
                    p_lanes = p_lanes + pb
                l_ref[h, rs] = alpha * l_ref[h, rs] + p_lanes
                for c, pb in enumerate(p_blocks):
                    p_ref[h, rs, c * LANES:(c + 1) * LANES] = pb.astype(p_ref.dtype)
                alpha_ref[h, rs, :] = alpha
                m_ref[h, rs] = m_new
            pv = jnp.dot(p_ref[h], v, preferred_element_type=F32)
            acc_ref[h] = alpha_ref[h] * acc_ref[h] + pv

    def body(j, carry):
        step(j, ["full"] * n_sub)
        return carry

    if mode == "none":
        lax.fori_loop(0, n_kv_blocks, body, 0)
    else:
        kv_per_q = qb // kb
        lax.fori_loop(0, qi * kv_per_q, body, 0)
        for t in range(kv_per_q):
            kinds = ["full" if r // per_kv > t else "diag" if r // per_kv == t else "none" for r in range(n_sub)]
            step(qi * kv_per_q + t, kinds)

    for h in range(hq):
        l = jnp.sum(l_ref[h], axis=1, keepdims=True)
        o_ref[0, :, h * dv:(h + 1) * dv] = (acc_ref[h] / l).astype(o_ref.dtype)


def flash_attention(q, k, v, *, n_heads, hq, hk, dk, dv, q_blk, k_blk, v_blk, scale, mode,
                    qb=512, kb=256, fox=None, bias=None):
    B, Sq = q.shape[0], q.shape[1]
    Skv = k.shape[1]
    kb = min(kb, Skv)
    nq, ng = Sq // qb, n_heads // hq
    assert hk in (hq, 1) and qb % kb == 0 and kb % FLASH_RB == 0
    if mode in ("causal", "chunk", "bias"):
        assert Sq == Skv
    if mode != "bias":
        grid = (B, ng, nq)
        q_axis = 2

        def spec(shape, fn):
            return pl.BlockSpec(shape, lambda b, g, i: fn(b, g, i))
    else:
        grid = (B, nq, ng)
        q_axis = 1

        def spec(shape, fn):
            return pl.BlockSpec(shape, lambda b, i, g: fn(b, g, i))

    in_specs = [
        spec((1, qb, hq * dk), lambda b, g, i: (b, i, q_blk + g)),
        spec((1, Skv, hk * dk), lambda b, g, i: (b, 0, k_blk + g)),
        spec((1, Skv, hk * dv), lambda b, g, i: (b, 0, v_blk + g)),
    ]
    args = [q, k, v]
    if fox is not None:
        ct, cs = fox
        in_specs += [spec((1, hq, qb, 1), lambda b, g, i: (b, g, i, 0)),
                     spec((1, hq, 1, Skv), lambda b, g, i: (b, g, 0, 0))]
        args += [ct, cs]
    if mode == "bias":
        in_specs.append(spec((1, qb, Skv), lambda b, g, i: (b, i, 0)))
        args.append(bias)
    assert dv == LANES
    stat = pltpu.VMEM((hq, qb, LANES), F32)
    kern = functools.partial(_flash_kernel, hq=hq, hk=hk, dk=dk, dv=dv, qb=qb, kb=kb, scale=scale, mode=mode,
                             fox=fox is not None, q_axis=q_axis, n_kv_blocks=Skv // kb)
    return pl.pallas_call(
        kern,
        grid=grid,
        in_specs=in_specs,
        out_specs=spec((1, qb, hq * dv), lambda b, g, i: (b, i, g)),
        out_shape=jax.ShapeDtypeStruct((B, Sq, n_heads * dv), BF16),
        scratch_shapes=[stat, stat, pltpu.VMEM((hq, qb, dv), F32), pltpu.VMEM((hq, qb, kb), BF16), stat]
        + ([stat] if fox is not None else []),
        compiler_params=_params(("parallel", "parallel", "arbitrary")),
        name="flash_" + mode,
    )(*args)


BAND_QB = 2 * CHUNK
BAND_KBLOCKS = (LEFT_CHUNKS * CHUNK) // BAND_QB + 1
BAND_W = BAND_KBLOCKS * BAND_QB


def band_bias_tiles(rel_bias):
    pad = LEFT_CHUNKS * CHUNK
    i = jnp.arange(BAND_QB)[:, None]
    j = jnp.arange(BAND_W)[None, :]
    rel = i + pad - j
    chunk_diff = i // CHUNK + LEFT_CHUNKS - j // CHUNK
    in_band = (chunk_diff >= 0) & (chunk_diff <= LEFT_CHUNKS)
    bias = rel_bias[:, jnp.clip(rel, -REL_CLIP, REL_CLIP) + REL_CLIP].astype(F32)
    return jnp.where(in_band[None], bias, MASK_NEG)


def _band_kernel(q_ref, k_ref, v_ref, bias_ref, o_ref, *, hb, scale):
    qi = pl.program_id(2)
    for h in range(hb):
        lanes = slice(h * HEAD_DIM, (h + 1) * HEAD_DIM)
        q = q_ref[0, :, lanes]
        s_parts, v_parts = [], []
        for jb in range(BAND_KBLOCKS):
            kblk = qi - (BAND_KBLOCKS - 1) + jb
            ks = pl.multiple_of(jnp.maximum(kblk, 0) * BAND_QB, BAND_QB)
            k = k_ref[0, pl.ds(ks, BAND_QB), lanes]
            v_parts.append(v_ref[0, pl.ds(ks, BAND_QB), lanes])
            s = lax.dot_general(q, k, (((1,), (1,)), ((), ())), preferred_element_type=F32) * scale
            s = s + bias_ref[h, :, jb * BAND_QB:(jb + 1) * BAND_QB]
            s_parts.append(jnp.where(kblk >= 0, s, MASK_NEG))
        s = jnp.concatenate(s_parts, axis=1)
        p = jnp.exp(s - jnp.max(s, axis=1, keepdims=True))
        l = jnp.sum(p, axis=1, keepdims=True)
        acc = jnp.zeros((BAND_QB, HEAD_DIM), F32)
        for jb in range(BAND_KBLOCKS):
            pj = p[:, jb * BAND_QB:(jb + 1) * BAND_QB].astype(v_parts[jb].dtype)
            acc = acc + jnp.dot(pj, v_parts[jb], preferred_element_type=F32)
        o_ref[0, :, lanes] = (acc / l).astype(o_ref.dtype)


def band_attention(qkv, bias_tiles, hb=4):
    B, S, _ = qkv.shape
    H = A_HEADS
    ng = H // hb
    return pl.pallas_call(
        functools.partial(_band_kernel, hb=hb, scale=HEAD_DIM ** -0.5),
        grid=(B, ng, S // BAND_QB),
        in_specs=[pl.BlockSpec((1, BAND_QB, hb * HEAD_DIM), lambda b, g, i: (b, i, g)),
                  pl.BlockSpec((1, S, hb * HEAD_DIM), lambda b, g, i: (b, 0, ng + g)),
                  pl.BlockSpec((1, S, hb * HEAD_DIM), lambda b, g, i: (b, 0, 2 * ng + g)),
                  pl.BlockSpec((hb, BAND_QB, BAND_W), lambda b, g, i: (g, 0, 0))],
        out_specs=pl.BlockSpec((1, BAND_QB, hb * HEAD_DIM), lambda b, g, i: (b, i, g)),
        out_shape=jax.ShapeDtypeStruct((B, S, H * HEAD_DIM), BF16),
        compiler_params=_params(("parallel", "parallel", "arbitrary")),
        name="band_attention",
    )(qkv, qkv, qkv, bias_tiles)


def _fox_cumsum_kernel(f_ref, b_ref, o_ref, carry_ref, *, cb):
    @pl.when(pl.program_id(1) == 0)
    def _():
        carry_ref[...] = jnp.zeros(carry_ref.shape, F32)

    x = f_ref[0].astype(F32) + b_ref[...]
    log_f = jnp.minimum(x, 0.0) - jnp.log1p(jnp.exp(-jnp.abs(x)))
    r = lax.broadcasted_iota(jnp.int32, (cb, cb), 0)
    c = lax.broadcasted_iota(jnp.int32, (cb, cb), 1)
    tri = jnp.where(c <= r, 1.0, 0.0).astype(F32)
    cum = jnp.dot(tri, log_f, preferred_element_type=F32, precision=lax.Precision.HIGHEST) + carry_ref[...]
    o_ref[0] = cum
    carry_ref[...] = cum[cb - 1:cb, :]


def fox_cumsum(proj, col_block, bias_row, batch, seq, cb=256):
    x = proj.reshape(batch, seq, proj.shape[-1])
    return pl.pallas_call(
        functools.partial(_fox_cumsum_kernel, cb=cb),
        grid=(batch, seq // cb),
        in_specs=[pl.BlockSpec((1, cb, LANES), lambda b, i: (b, i, col_block)),
                  pl.BlockSpec((1, LANES), lambda b, i: (0, 0))],
        out_specs=pl.BlockSpec((1, cb, LANES), lambda b, i: (b, i, 0)),
        out_shape=jax.ShapeDtypeStruct((batch, seq, LANES), F32),
        scratch_shapes=[pltpu.VMEM((1, LANES), F32)],
        compiler_params=_params(("parallel", "arbitrary")),
        name="fox_cumsum",
    )(x, bias_row)


IDX_QB = 128
IDX_KC = 512


def _sortable_key(x):
    bits = pltpu.bitcast(x, jnp.int32)
    return bits ^ ((bits >> 31) & jnp.int32(0x7FFFFFFF))


def _indexer_kernel(ki_ref, qit_ref, w_ref, o_ref, key_ref, *, seq, topk):
    qi = pl.program_id(1)
    q0 = qi * IDX_QB
    n_chunks = (q0 + IDX_QB + IDX_KC - 1) // IDX_KC
    qchunk = (q0 + lax.broadcasted_iota(jnp.int32, (1, IDX_QB), 1)) // CHUNK
    int_min = jnp.int32(-2 ** 31)
    w_scale = IDX_HEADS ** -0.5 * IDX_HD ** -0.5

    def score_chunk(c, carry):
        ks = pl.multiple_of(c * IDX_KC, IDX_KC)
        ki = ki_ref[0, pl.ds(ks, IDX_KC), :][:, :IDX_HD]
        acc = jnp.zeros((IDX_KC, IDX_QB), F32)
        for hp in range(IDX_HEADS // 2):
            t = jnp.dot(ki, qit_ref[0, 0, :, hp * 2 * IDX_QB:(hp + 1) * 2 * IDX_QB], preferred_element_type=F32)
            w = w_ref[0, 0, :, hp * 2 * IDX_QB:(hp + 1) * 2 * IDX_QB] * w_scale
            t = jnp.maximum(t, 0.0) * w
            acc = acc + t[:, :IDX_QB] + t[:, IDX_QB:]
        kchunk = (ks + lax.broadcasted_iota(jnp.int32, (IDX_KC, 1), 0)) // CHUNK
        acc = jnp.where(kchunk <= qchunk, acc, -jnp.inf)
        key_ref[pl.ds(ks, IDX_KC), :] = _sortable_key(acc)
        return carry

    lax.fori_loop(0, n_chunks, score_chunk, 0)

    def count_ge(cand):
        def body(c, acc):
            ks = pl.multiple_of(c * IDX_KC, IDX_KC)
            blk = key_ref[pl.ds(ks, IDX_KC), :]
            hit = jnp.where(blk >= cand, 1, 0).astype(jnp.int32)
            return acc + jnp.sum(hit.reshape(IDX_KC // 8, 8, IDX_QB), axis=0)
        acc = lax.fori_loop(0, n_chunks, body, jnp.zeros((8, IDX_QB), jnp.int32))
        return jnp.sum(acc, axis=0, keepdims=True)

    def bit_step(i, ans):
        bit = lax.shift_left(jnp.int32(1), 31 - i)
        cand = ans | bit
        cnt = count_ge(cand ^ int_min)
        return jnp.where(cnt >= topk, cand, ans)

    ans = lax.fori_loop(0, 32, bit_step, jnp.zeros((1, IDX_QB), jnp.int32))
    thr = ans ^ int_min
    neg_inf_key = _sortable_key(jnp.full((1, IDX_QB), -jnp.inf, F32))

    def emit(c, carry):
        ks = pl.multiple_of(c * IDX_KC, IDX_KC)
        blk = key_ref[pl.ds(ks, IDX_KC), :]
        sel = (blk >= thr) & (blk > neg_inf_key)
        bias_t = jnp.where(sel, 0.0, MASK_NEG).astype(F32)
        o_ref[0, :, pl.ds(ks, IDX_KC)] = bias_t.T.astype(o_ref.dtype)
        return carry

    lax.fori_loop(0, n_chunks, emit, 0)

    def fill(c, carry):
        ks = pl.multiple_of(c * IDX_KC, IDX_KC)
        o_ref[0, :, pl.ds(ks, IDX_KC)] = jnp.full((IDX_QB, IDX_KC), MASK_NEG, o_ref.dtype)
        return carry

    lax.fori_loop(n_chunks, seq // IDX_KC, fill, 0)


def dsa_selection_bias(ki, qit, wt, topk):
    B, S, _ = ki.shape
    nq = S // IDX_QB
    return pl.pallas_call(
        functools.partial(_indexer_kernel, seq=S, topk=topk),
        grid=(B, nq),
        in_specs=[pl.BlockSpec((1, S, LANES), lambda b, i: (b, 0, 0)),
                  pl.BlockSpec((1, 1, IDX_HD, IDX_HEADS * IDX_QB), lambda b, i: (b, i, 0, 0)),
                  pl.BlockSpec((1, 1, 1, IDX_HEADS * IDX_QB), lambda b, i: (b, i, 0, 0))],
        out_specs=pl.BlockSpec((1, IDX_QB, S), lambda b, i: (b, i, 0)),
        out_shape=jax.ShapeDtypeStruct((B, S, S), BF16),
        scratch_shapes=[pltpu.VMEM((S, IDX_QB), jnp.int32)],
        compiler_params=_params(("parallel", "arbitrary")),
        name="dsa_indexer",
    )(ki, qit, wt)


PEER_TB_ROUTE = 128
STAT_ROWS = 8


def _peer_route_kernel(qt_ref, sk_ref, s_ref, st_ref, top_ref):
    half = PEER_DKEY // 2
    tb = PEER_TB_ROUTE
    for hp in range(2 * PEER_HEADS):
        q = qt_ref[hp * half:(hp + 1) * half, :].astype(BF16)
        s_ref[hp] = jnp.dot(sk_ref[hp], q, preferred_element_type=F32)

    def head_stats(h, carry):
        def extract_pair(i, c):
            xa, xb = c
            ma = jnp.max(xa, axis=0, keepdims=True)
            mb = jnp.max(xb, axis=0, keepdims=True)
            top_ref[0, pl.ds(i, 1), :] = ma
            top_ref[1, pl.ds(i, 1), :] = mb
            return jnp.where(xa == ma, -jnp.inf, xa), jnp.where(xb == mb, -jnp.inf, xb)

        lax.fori_loop(0, PEER_TOPK, extract_pair, (s_ref[2 * h], s_ref[2 * h + 1]))
        a = top_ref[0]
        b = top_ref[1]
        cand = jnp.concatenate([a[0:1, :] + b] + [a[i:i + 1, :] + b[0:8, :] for i in range(1, 8)]
                               + [a[8:PEER_TOPK, :] + b[0:1, :]], axis=0)

        def extract(i, c):
            x, mx, z, _ = c
            m = jnp.max(x, axis=0, keepdims=True)
            mx = jnp.where(i == 0, m, mx)
            z = z + jnp.exp(m - mx)
            return jnp.where(x == m, -jnp.inf, x), mx, z, m

        zero = jnp.zeros((1, tb), F32)
        _, mx, z, last = lax.fori_loop(0, PEER_TOPK, extract, (cand, zero, zero, zero))
        st_ref[h, 0:1, :] = last
        st_ref[h, 1:2, :] = a[0:1, :]
        st_ref[h, 2:3, :] = b[0:1, :]
        st_ref[h, 3:4, :] = 1.0 / z
        st_ref[h, 4:STAT_ROWS, :] = jnp.zeros((STAT_ROWS - 4, tb), F32)
        return carry

    lax.fori_loop(0, PEER_HEADS, head_stats, 0)


def peer_route(qt, sub_keys):
    R, T = qt.shape
    tb = PEER_TB_ROUTE
    nsub = 2 * PEER_HEADS
    return pl.pallas_call(
        _peer_route_kernel,
        grid=(T // tb,),
        in_specs=[pl.BlockSpec((R, tb), lambda i: (0, i)),
                  pl.BlockSpec((nsub, N_KEYS, PEER_DKEY // 2), lambda i: (0, 0, 0))],
        out_specs=[pl.BlockSpec((nsub, N_KEYS, tb), lambda i: (0, 0, i)),
                   pl.BlockSpec((PEER_HEADS, STAT_ROWS, tb), lambda i: (0, 0, i))],
        out_shape=[jax.ShapeDtypeStruct((nsub, N_KEYS, T), F32),
                   jax.ShapeDtypeStruct((PEER_HEADS, STAT_ROWS, T), F32)],
        scratch_shapes=[pltpu.VMEM((2, PEER_TOPK, tb), F32)],
        compiler_params=_params(("parallel",)),
        name="peer_route",
    )(qt, sub_keys)


PEER_TB = 512
PEER_EB = 512


def _gelu_exact(x):
    return 0.5 * x * (1.0 + lax.erf(x * (2.0 ** -0.5)))


def _peer_dense_kernel(xt_ref, u_ref, v_ref, s_ref, st_ref, y_ref, e2_ref, ht_ref):
    e = pl.program_id(1)

    @pl.when(e == 0)
    def _():
        y_ref[...] = jnp.zeros(y_ref.shape, F32)
        for h in range(PEER_HEADS):
            e2_ref[h] = jnp.exp(s_ref[2 * h + 1] - st_ref[h, 2:3, :])

    act = _gelu_exact(jnp.dot(u_ref[...], xt_ref[...], preferred_element_type=F32))
    rows_per_step = PEER_EB // N_KEYS
    for r in range(rows_per_step):
        i1 = e * rows_per_step + r
        gate = jnp.zeros((N_KEYS, PEER_TB), F32)
        for h in range(PEER_HEADS):
            s1 = s_ref[2 * h, pl.ds(i1, 1), :]
            e1 = jnp.exp(s1 - st_ref[h, 1:2, :]) * st_ref[h, 3:4, :]
            pair = s_ref[2 * h + 1] + s1
            gate = gate + jnp.where(pair >= st_ref[h, 0:1, :], e2_ref[h] * e1, 0.0)
        ht_ref[r * N_KEYS:(r + 1) * N_KEYS, :] = (gate * act[r * N_KEYS:(r + 1) * N_KEYS, :]).astype(BF16)
    y_ref[...] += lax.dot_general(ht_ref[...], v_ref[...], (((0,), (0,)), ((), ())), preferred_element_type=F32)


def peer_dense(xt, u, v, s_t, stats):
    D, T = xt.shape
    E = u.shape[0]
    nsub = 2 * PEER_HEADS
    once = pl.Buffered(1)
    return pl.pallas_call(
        _peer_dense_kernel,
        grid=(T // PEER_TB, E // PEER_EB),
        in_specs=[pl.BlockSpec((D, PEER_TB), lambda i, e: (0, i), pipeline_mode=once),
                  pl.BlockSpec((PEER_EB, D), lambda i, e: (e, 0)),
                  pl.BlockSpec((PEER_EB, D), lambda i, e: (e, 0)),
                  pl.BlockSpec((nsub, N_KEYS, PEER_TB), lambda i, e: (0, 0, i), pipeline_mode=once),
                  pl.BlockSpec((PEER_HEADS, STAT_ROWS, PEER_TB), lambda i, e: (0, 0, i))],
        out_specs=pl.BlockSpec((PEER_TB, D), lambda i, e: (i, 0)),
        out_shape=jax.ShapeDtypeStruct((T, D), F32),
        scratch_shapes=[pltpu.VMEM((PEER_HEADS, N_KEYS, PEER_TB), F32), pltpu.VMEM((PEER_EB, PEER_TB), BF16)],
        compiler_params=_params(("parallel", "arbitrary")),
        name="peer_dense",
    )(xt, u, v, s_t, stats)


def _pad_cols(w, width):
    return jnp.pad(w, ((0, 0), (0, width - w.shape[1])))


def mixer_ab(hb, batch, seq, w_in, rel_bias, q_norm, w_uq, kv_norm, w_ukv, w_out):
    T, D = hb.shape
    a_w = A_HEADS * HEAD_DIM
    w_qkv = w_in[:, :3 * a_w].astype(BF16)
    c_cols = MLA_Q_LORA + MLA_KV_LORA + LANES
    w_c = _pad_cols(w_in[:, 3 * a_w:], _round_up(c_cols, 512)).astype(BF16)
    qkv = matmul(hb, w_qkv, BF16)
    cproj = matmul(hb, w_c, F32)

    oa = band_attention(qkv.reshape(batch, seq, 3 * a_w), band_bias_tiles(rel_bias))

    qh = MLA_NOPE + MLA_ROPE
    w_uq_p = jnp.pad(w_uq.reshape(MLA_Q_LORA, B_HEADS, qh), ((0, 0), (0, 0), (0, 2 * LANES - qh)))
    w_uq_p = w_uq_p.reshape(MLA_Q_LORA, B_HEADS * 2 * LANES).astype(BF16)
    cq_n = rms_norm_cols(cproj, 0, MLA_Q_LORA, q_norm)
    ckv_n = rms_norm_cols(cproj, MLA_Q_LORA // MLA_KV_LORA, MLA_KV_LORA, kv_norm)
    q_lat = matmul(cq_n, w_uq_p, F32)
    q_cat = rope_cols(q_lat, 0, B_HEADS, 2 * LANES, MLA_ROPE // 2,
                      rope_tables(seq, MLA_ROPE, 2 * LANES, offset=MLA_NOPE), seq)
    kv = matmul(ckv_n, w_ukv.astype(BF16), BF16)
    k_cat, v_cat = mla_assemble_kv(kv, cproj, (MLA_Q_LORA + MLA_KV_LORA) // LANES,
                                   rope_tables(seq, MLA_ROPE, LANES), seq)
    ob = flash_attention(q_cat.reshape(batch, seq, -1), k_cat.reshape(batch, seq, -1), v_cat.reshape(batch, seq, -1),
                         n_heads=B_HEADS, hq=2, hk=2, dk=2 * LANES, dv=MLA_V, q_blk=0, k_blk=0, v_blk=0,
                         scale=(MLA_NOPE + MLA_ROPE) ** -0.5, mode="chunk")
    o = jnp.concatenate([oa.reshape(T, -1), ob.reshape(T, -1)], axis=-1)
    return matmul(o, w_out.astype(BF16), F32)


def mixer_cd(hb, batch, seq, w_in, forget_bias, w_out):
    T, D = hb.shape
    c_qw, c_kw, d_w = C_HEADS * HEAD_DIM, C_KV_HEADS * HEAD_DIM, D_HEADS * HEAD_DIM
    i_w = IDX_HEADS * IDX_HD
    offs = np.cumsum([0, c_qw, c_kw, c_kw, i_w, IDX_HD, IDX_HEADS, d_w, d_w, d_w, D_HEADS])
    col = lambda n: w_in[:, offs[n]:offs[n + 1]]
    w_a = jnp.concatenate([col(0), col(1), col(2), col(6), col(7), col(8)], axis=1).astype(BF16)
    w_b = jnp.concatenate([col(3), _pad_cols(col(4), LANES), _pad_cols(col(5), LANES), _pad_cols(col(9), LANES)], axis=1)
    w_b = _pad_cols(w_b, _round_up(w_b.shape[1], 512)).astype(BF16)
    proj_a = matmul(hb, w_a, BF16)
    proj_b = matmul(hb, w_b, F32)
    qk_w = c_qw + c_kw
    blk_ki, blk_wi, blk_fd = i_w // LANES, i_w // LANES + 1, i_w // LANES + 2

    qk_rot = rope_cols(proj_a, 0, qk_w // LANES, LANES, HEAD_DIM // 2, rope_tables(seq, HEAD_DIM, LANES), seq)
    qi_rot = rope_cols(proj_b, 0, i_w // LANES, LANES, IDX_HD // 2, rope_tables(seq, IDX_HD, LANES, reps=2), seq)
    ki_rot = rope_cols(proj_b, blk_ki, 1, LANES, IDX_HD // 2, rope_tables(seq, IDX_HD, LANES), seq)
    nq = seq // IDX_QB
    qit = qi_rot.reshape(batch, nq, IDX_QB, IDX_HEADS, IDX_HD).transpose(0, 1, 4, 3, 2)
    qit = qit.reshape(batch, nq, IDX_HD, IDX_HEADS * IDX_QB)
    wi = proj_b[:, blk_wi * LANES:blk_wi * LANES + IDX_HEADS]
    wt = wi.reshape(batch, nq, IDX_QB, IDX_HEADS).transpose(0, 1, 3, 2).reshape(batch, nq, 1, IDX_HEADS * IDX_QB)
    sel_bias = dsa_selection_bias(ki_rot.reshape(batch, seq, LANES), qit, wt, min(DSA_TOPK_MAX, seq // 4))
    qk3 = qk_rot.reshape(batch, seq, qk_w)
    pa3 = proj_a.reshape(batch, seq, -1)
    rep = C_HEADS // C_KV_HEADS
    oc = flash_attention(qk3, qk3, pa3, n_heads=C_HEADS, hq=rep, hk=1, dk=HEAD_DIM, dv=HEAD_DIM,
                         q_blk=0, k_blk=c_qw // HEAD_DIM, v_blk=qk_w // HEAD_DIM,
                         scale=HEAD_DIM ** -0.5, mode="bias", bias=sel_bias)

    fbias = _pad_cols(forget_bias.reshape(1, D_HEADS).astype(F32), LANES)
    cum = fox_cumsum(proj_b, blk_fd, fbias, batch, seq)[:, :, :D_HEADS]
    cum_t = cum.transpose(0, 2, 1)
    hd = 2
    d0 = (qk_w + c_kw) // (hd * HEAD_DIM)
    od = flash_attention(pa3, pa3, pa3, n_heads=D_HEADS, hq=hd, hk=hd, dk=HEAD_DIM, dv=HEAD_DIM,
                         q_blk=d0, k_blk=d0 + d_w // (hd * HEAD_DIM), v_blk=d0 + 2 * d_w // (hd * HEAD_DIM),
                         scale=HEAD_DIM ** -0.5, mode="causal", fox=(cum_t[..., None], cum_t[:, :, None, :]))
    o = jnp.concatenate([oc.reshape(T, -1), od.reshape(T, -1)], axis=-1)
    return matmul(o, w_out.astype(BF16), F32)


def memory_cross_attention(hb, mem_b, batch, seq, w_q, w_kv, w_o):
    T, D = hb.shape
    mem_w = MEM_HEADS * MEM_HD
    q = matmul(hb, w_q.astype(BF16), BF16)
    kv = matmul(mem_b.reshape(-1, D), w_kv.astype(BF16), BF16)
    kv3 = kv.reshape(batch, -1, 2 * mem_w)
    o = flash_attention(q.reshape(batch, seq, mem_w), kv3, kv3, n_heads=MEM_HEADS, hq=MEM_HEADS, hk=MEM_HEADS,
                        dk=MEM_HD, dv=MEM_HD, q_blk=0, k_blk=0, v_blk=1, scale=MEM_HD ** -0.5, mode="none",
                        kb=kv3.shape[1])
    return matmul(o.reshape(T, mem_w), w_o.astype(BF16), F32)


def peer_ffn(hb, w_q, sub_keys, u_tab, v_tab):
    ht = hb.T
    qt = matmul(w_q.T.astype(BF16), ht, F32)
    sk = sub_keys.reshape(2 * PEER_HEADS, N_KEYS, PEER_DKEY // 2).astype(BF16)
    s_t, stats = peer_route(qt, sk)
    return peer_dense(ht, u_tab.astype(BF16), v_tab.astype(BF16), s_t, stats)


def kernel(x, mem, ab_w_in, a_rel_bias, b_q_norm, b_w_uq, b_kv_norm, b_w_ukv, ab_w_out, cd_w_in, d_forget_bias,
           cd_w_out, mem_w_q, mem_w_kv, mem_w_o, peer_w_q, peer_sub_keys, peer_u, peer_v, ln_g, ln_b):
    batch, seq, d_model = x.shape
    h = x.reshape(batch * seq, d_model)
    hb = h.astype(BF16)
    mem_b = mem.astype(BF16)
    for layer in range(DEPTH):
        j = layer // 2
        if layer % 2 == 0:
            y = mixer_ab(hb, batch, seq, ab_w_in[j], a_rel_bias[j], b_q_norm[j], b_w_uq[j], b_kv_norm[j],
                         b_w_ukv[j], ab_w_out[j])
        else:
            y = mixer_cd(hb, batch, seq, cd_w_in[j], d_forget_bias[j], cd_w_out[j])
        h, hb = deepnorm_ln(h, y, ln_g[layer, 0], ln_b[layer, 0])
        y = memory_cross_attention(hb, mem_b, batch, seq, mem_w_q[layer], mem_w_kv[layer], mem_w_o[layer])
        h, hb = deepnorm_ln(h, y, ln_g[layer, 1], ln_b[layer, 1])
        y = peer_ffn(hb, peer_w_q[layer], peer_sub_keys[layer], peer_u[layer], peer_v[layer])
        h, hb = deepnorm_ln(h, y, ln_g[layer, 2], ln_b[layer, 2])
    return h.reshape(batch, seq, d_model)
```

```python
import functools

import jax
import jax.numpy as jnp
import numpy as np
from jax import lax
from jax.experimental import pallas as pl
from jax.experimental.pallas import tpu as pltpu

F32 = jnp.float32
BF16 = jnp.bfloat16

CHUNK = 64
HEAD_DIM = 128
A_HEADS = 16
LEFT_CHUNKS = 8
REL_CLIP = 256
B_HEADS = 16
MLA_Q_LORA = 1536
MLA_KV_LORA = 512
MLA_NOPE = 128
MLA_ROPE = 64
MLA_V = 128
C_HEADS = 16
C_KV_HEADS = 4
IDX_HEADS = 32
IDX_HD = 64
DSA_TOPK_MAX = 256
D_HEADS = 16
MEM_HEADS = 4
MEM_HD = 128
PEER_HEADS = 8
PEER_DKEY = 256
N_KEYS = 128
PEER_TOPK = 16
ROPE_THETA = 10000.0
LN_EPS = 1e-5
RMS_EPS = 1e-6
DEPTH = 2
DEEPNORM_ALPHA = (2 * DEPTH) ** 0.25

LANES = 128
V7X_VMEM_LIMIT = 56 * 1024 * 1024
MASK_NEG = -1e30
LOG2E = 1.4426950408889634


def _params(sem, vmem=V7X_VMEM_LIMIT, flags=None):
    return pltpu.CompilerParams(dimension_semantics=sem, vmem_limit_bytes=vmem, flags=flags)


def _round_up(n, m):
    return (n + m - 1) // m * m


def _mm_kernel(a_ref, w_ref, o_ref):
    o_ref[...] = jnp.dot(a_ref[...], w_ref[...], preferred_element_type=F32).astype(o_ref.dtype)


def matmul(a, w, out_dtype, tm=1024, tn=512):
    M, K = a.shape
    N = w.shape[1]
    tm = min(tm, M)
    tn = min(tn, N)
    assert M % tm == 0 and N % tn == 0, (M, N, tm, tn)
    return pl.pallas_call(
        _mm_kernel,
        grid=(M // tm, N // tn),
        in_specs=[pl.BlockSpec((tm, K), lambda i, j: (i, 0)),
                  pl.BlockSpec((K, tn), lambda i, j: (0, j))],
        out_specs=pl.BlockSpec((tm, tn), lambda i, j: (i, j)),
        out_shape=jax.ShapeDtypeStruct((M, N), out_dtype),
        compiler_params=_params(("parallel", "arbitrary")),
        name="matmul",
    )(a, w)


def _mm2_kernel(a1_ref, a2_ref, w_ref, o_ref):
    a = jnp.concatenate([a1_ref[...], a2_ref[...]], axis=1)
    o_ref[...] = jnp.dot(a, w_ref[...], preferred_element_type=F32).astype(o_ref.dtype)


def matmul_cat(a1, a2, w, out_dtype, tm=1024, tn=512):
    M, K1 = a1.shape
    K2 = a2.shape[1]
    N = w.shape[1]
    tm = min(tm, M)
    tn = min(tn, N)
    assert M % tm == 0 and N % tn == 0 and w.shape[0] == K1 + K2
    return pl.pallas_call(
        _mm2_kernel,
        grid=(M // tm, N // tn),
        in_specs=[pl.BlockSpec((tm, K1), lambda i, j: (i, 0)),
                  pl.BlockSpec((tm, K2), lambda i, j: (i, 0)),
                  pl.BlockSpec((K1 + K2, tn), lambda i, j: (0, j))],
        out_specs=pl.BlockSpec((tm, tn), lambda i, j: (i, j)),
        out_shape=jax.ShapeDtypeStruct((M, N), out_dtype),
        compiler_params=_params(("parallel", "arbitrary")),
        name="matmul_cat",
    )(a1, a2, w)


def _cast_kernel(w_ref, o_ref):
    o_ref[...] = w_ref[...].astype(o_ref.dtype)


def cast_layer(w, layer, tr=512):
    _, R, C = w.shape
    return pl.pallas_call(
        _cast_kernel,
        grid=(R // tr,),
        in_specs=[pl.BlockSpec((None, tr, C), lambda i: (layer, i, 0))],
        out_specs=pl.BlockSpec((tr, C), lambda i: (i, 0)),
        out_shape=jax.ShapeDtypeStruct((R, C), BF16),
        compiler_params=_params(("parallel",)),
        name="cast_layer",
    )(w)


def _ln_kernel(h_ref, y_ref, g_ref, b_ref, o_ref, ob_ref):
    z = DEEPNORM_ALPHA * h_ref[...] + y_ref[...]
    zc = z - jnp.mean(z, axis=-1, keepdims=True)
    var = jnp.mean(zc * zc, axis=-1, keepdims=True)
    out = zc * lax.rsqrt(var + LN_EPS) * g_ref[...] + b_ref[...]
    o_ref[...] = out
    ob_ref[...] = out.astype(BF16)


def deepnorm_ln(h, y, g, b, tb=256):
    T, D = h.shape
    row = pl.BlockSpec((tb, D), lambda i: (i, 0))
    vec = pl.BlockSpec((1, D), lambda i: (0, 0))
    return pl.pallas_call(
        _ln_kernel,
        grid=(T // tb,),
        in_specs=[row, row, vec, vec],
        out_specs=[row, row],
        out_shape=[jax.ShapeDtypeStruct((T, D), F32), jax.ShapeDtypeStruct((T, D), BF16)],
        compiler_params=_params(("parallel",)),
        name="deepnorm_ln",
    )(h, y, g.reshape(1, D), b.reshape(1, D))


def _rms_kernel(x_ref, g_ref, o_ref):
    x = x_ref[...].astype(F32)
    o_ref[...] = (x * lax.rsqrt(jnp.mean(x * x, axis=-1, keepdims=True) + RMS_EPS) * g_ref[...]).astype(o_ref.dtype)


def rms_norm_cols(x, col_block, width, g, tb=512):
    T = x.shape[0]
    return pl.pallas_call(
        _rms_kernel,
        grid=(T // tb,),
        in_specs=[pl.BlockSpec((tb, width), lambda i: (i, col_block)),
                  pl.BlockSpec((1, width), lambda i: (0, 0))],
        out_specs=pl.BlockSpec((tb, width), lambda i: (i, 0)),
        out_shape=jax.ShapeDtypeStruct((T, width), BF16),
        compiler_params=_params(("parallel",)),
        name="rms_norm",
    )(x, g.reshape(1, width))


def rope_tables(seq, dim, group, offset=0, reps=1, scale=1.0):
    half = dim // 2
    inv = ROPE_THETA ** (-jnp.arange(half, dtype=F32) * 2.0 / dim)
    ang = jnp.arange(seq, dtype=F32)[:, None] * inv[None, :]
    cos, sin = jnp.cos(ang), jnp.sin(ang)
    zeros_tail = jnp.zeros((seq, group - offset - dim * reps), F32)
    head = jnp.ones((seq, offset), F32)
    zhead = jnp.zeros((seq, offset), F32)
    zhalf = jnp.zeros((seq, half), F32)
    c = jnp.concatenate([head] + [cos, cos] * reps + [zeros_tail], axis=1)
    sa = jnp.concatenate([zhead] + [-sin, zhalf] * reps + [zeros_tail], axis=1)
    sb = jnp.concatenate([zhead] + [zhalf, sin] * reps + [zeros_tail], axis=1)
    return c * scale, sa * scale, sb * scale


def _rope_kernel(x_ref, c_ref, sa_ref, sb_ref, o_ref, *, groups, group, half):
    c, sa, sb = c_ref[...], sa_ref[...], sb_ref[...]
    for g in range(groups):
        x = x_ref[:, g * group:(g + 1) * group].astype(F32)
        out = x * c + pltpu.roll(x, group - half, 1) * sa + pltpu.roll(x, half, 1) * sb
        o_ref[:, g * group:(g + 1) * group] = out.astype(o_ref.dtype)


def rope_cols(x, col_block, groups, group, half, tables, seq, tb=256):
    T = x.shape[0]
    W = groups * group
    nsb = seq // tb
    tab = pl.BlockSpec((tb, group), lambda i: (i % nsb, 0))
    return pl.pallas_call(
        functools.partial(_rope_kernel, groups=groups, group=group, half=half),
        grid=(T // tb,),
        in_specs=[pl.BlockSpec((tb, W), lambda i: (i, col_block)), tab, tab, tab],
        out_specs=pl.BlockSpec((tb, W), lambda i: (i, 0)),
        out_shape=jax.ShapeDtypeStruct((T, W), BF16),
        compiler_params=_params(("parallel",)),
        name="rope",
    )(x, *tables)


def _mla_kv_kernel(kv_ref, kr_ref, c_ref, sa_ref, sb_ref, k_ref, v_ref, *, heads, half):
    kr = kr_ref[...].astype(F32)
    kr = kr * c_ref[...] + pltpu.roll(kr, LANES - half, 1) * sa_ref[...] + pltpu.roll(kr, half, 1) * sb_ref[...]
    kr = kr.astype(k_ref.dtype)
    for h in range(heads):
        k_ref[:, 2 * h * LANES:(2 * h + 1) * LANES] = kv_ref[:, 2 * h * LANES:(2 * h + 1) * LANES]
        k_ref[:, (2 * h + 1) * LANES:(2 * h + 2) * LANES] = kr
        v_ref[:, h * LANES:(h + 1) * LANES] = kv_ref[:, (2 * h + 1) * LANES:(2 * h + 2) * LANES]


def mla_assemble_kv(kv, cproj, kr_col_block, tables, seq, tb=256):
    T, W = kv.shape
    nsb = seq // tb
    tab = pl.BlockSpec((tb, LANES), lambda i: (i % nsb, 0))
    return pl.pallas_call(
        functools.partial(_mla_kv_kernel, heads=B_HEADS, half=MLA_ROPE // 2),
        grid=(T // tb,),
        in_specs=[pl.BlockSpec((tb, W), lambda i: (i, 0)),
                  pl.BlockSpec((tb, LANES), lambda i: (i, kr_col_block)), tab, tab, tab],
        out_specs=[pl.BlockSpec((tb, W), lambda i: (i, 0)), pl.BlockSpec((tb, W // 2), lambda i: (i, 0))],
        out_shape=[jax.ShapeDtypeStruct((T, W), BF16), jax.ShapeDtypeStruct((T, W // 2), BF16)],
        compiler_params=_params(("parallel",)),
        name="mla_assemble_kv",
    )(kv, cproj, *tables)


FLASH_RB = 128


def _flash_kernel(*refs, hq, hk, dk, dv, qb, kb, mode, fox, q_axis, n_kv_blocks):
    it = iter(refs)
    q_ref, k_ref, v_ref = next(it), next(it), next(it)
    ct_ref = cs_ref = bias_ref = None
    if fox:
        ct_ref, cs_ref = next(it), next(it)
    if mode == "bias":
        bias_ref = next(it)
    o_ref, m_ref, l_ref, acc_ref, p_ref, alpha_ref = next(it), next(it), next(it), next(it), next(it), next(it)
    ct_lanes_ref = next(it) if fox else None

    qi = pl.program_id(q_axis)
    m_ref[...] = jnp.full(m_ref.shape, MASK_NEG, F32)
    l_ref[...] = jnp.zeros(l_ref.shape, F32)
    acc_ref[...] = jnp.zeros(acc_ref.shape, F32)
    if fox:
        for h in range(hq):
            ct_lanes_ref[h] = jnp.broadcast_to(ct_ref[0, h] * LOG2E, (qb, LANES))

    n_sub = qb // FLASH_RB
    per_kv = kb // FLASH_RB
    rows = lax.broadcasted_iota(jnp.int32, (FLASH_RB, kb), 0)
    cols = lax.broadcasted_iota(jnp.int32, (FLASH_RB, kb), 1)

    def diag_mask(r):
        row_in_kv = rows + (r % per_kv) * FLASH_RB
        if mode == "causal":
            return cols <= row_in_kv
        if mode == "chunk":
            return (cols // CHUNK) <= (row_in_kv // CHUNK)
        return None

    def step(j, visible):
        ks = pl.multiple_of(j * kb, kb)
        for h in range(hq):
            g = h if hk == hq else 0
            k = k_ref[0, pl.ds(ks, kb), g * dk:(g + 1) * dk]
            v = v_ref[0, pl.ds(ks, kb), g * dv:(g + 1) * dv]
            s_all = lax.dot_general(q_ref[0, :, h * dk:(h + 1) * dk], k, (((1,), (1,)), ((), ())),
                                    preferred_element_type=F32)
            cs = cs_ref[0, h, :, pl.ds(ks, kb)] * LOG2E if fox else None
            for r in range(n_sub):
                rs = slice(r * FLASH_RB, (r + 1) * FLASH_RB)
                if visible[r] == "none":
                    p_ref[h, rs, :] = jnp.zeros((FLASH_RB, kb), p_ref.dtype)
                    alpha_ref[h, rs, :] = jnp.ones((FLASH_RB, LANES), F32)
                    continue
                s = s_all[rs]
                if bias_ref is not None:
                    s = s + bias_ref[0, rs, pl.ds(ks, kb)].astype(F32)
                blocks = [s[:, c * LANES:(c + 1) * LANES] for c in range(kb // LANES)]
                if fox:
                    ct = ct_lanes_ref[h, rs]
                    blocks = [blk + ct - cs[:, c * LANES:(c + 1) * LANES] for c, blk in enumerate(blocks)]
                if visible[r] == "diag" and mode in ("causal", "chunk"):
                    mask = diag_mask(r)
                    blocks = [jnp.where(mask[:, c * LANES:(c + 1) * LANES], blk, MASK_NEG)
                              for c, blk in enumerate(blocks)]
                blk_max = blocks[0]
                for blk in blocks[1:]:
                    blk_max = jnp.maximum(blk_max, blk)
                m_prev = m_ref[h, rs]
                m_new = jnp.maximum(m_prev, jnp.max(blk_max, axis=1, keepdims=True))
                alpha = jnp.exp2(m_prev - m_new)
                p_blocks = [jnp.exp2(blk - m_new) for blk in blocks]
                p_lanes = p_blocks[0]
                for pb in p_blocks[1:]:
                    p_lanes = p_lanes + pb
                l_ref[h, rs] = alpha * l_ref[h, rs] + p_lanes
                for c, pb in enumerate(p_blocks):
                    p_ref[h, rs, c * LANES:(c + 1) * LANES] = pb.astype(p_ref.dtype)
                alpha_ref[h, rs, :] = alpha
                m_ref[h, rs] = m_new
            pv = jnp.dot(p_ref[h], v, preferred_element_type=F32)
            acc_ref[h] = alpha_ref[h] * acc_ref[h] + pv

    def body(j, carry):
        step(j, ["full"] * n_sub)
        return carry

    if mode == "none":
        lax.fori_loop(0, n_kv_blocks, body, 0)
    else:
        kv_per_q = qb // kb
        lax.fori_loop(0, qi * kv_per_q, body, 0)
        for t in range(kv_per_q):
            kinds = ["full" if r // per_kv > t else "diag" if r // per_kv == t else "none" for r in range(n_sub)]
            step(qi * kv_per_q + t, kinds)

    for h in range(hq):
        l = jnp.sum(l_ref[h], axis=1, keepdims=True)
        o_ref[0, :, h * dv:(h + 1) * dv] = (acc_ref[h] / l).astype(o_ref.dtype)


def flash_attention(q, k, v, *, n_heads, hq, hk, dk, dv, q_blk, k_blk, v_blk, mode,
                    qb=512, kb=256, fox=None, bias=None):
    B, Sq = q.shape[0], q.shape[1]
    Skv = k.shape[1]
    kb = min(kb, Skv)
    nq, ng = Sq // qb, n_heads // hq
    assert hk in (hq, 1) and qb % kb == 0 and kb % FLASH_RB == 0
    if mode in ("causal", "chunk", "bias"):
        assert Sq == Skv
    if mode != "bias":
        grid = (B, ng, nq)
        q_axis = 2

        def spec(shape, fn):
            return pl.BlockSpec(shape, lambda b, g, i: fn(b, g, i))
    else:
        grid = (B, nq, ng)
        q_axis = 1

        def spec(shape, fn):
            return pl.BlockSpec(shape, lambda b, i, g: fn(b, g, i))

    in_specs = [
        spec((1, qb, hq * dk), lambda b, g, i: (b, i, q_blk + g)),
        spec((1, Skv, hk * dk), lambda b, g, i: (b, 0, k_blk + g)),
        spec((1, Skv, hk * dv), lambda b, g, i: (b, 0, v_blk + g)),
    ]
    args = [q, k, v]
    if fox is not None:
        ct, cs = fox
        in_specs += [spec((1, hq, qb, 1), lambda b, g, i: (b, g, i, 0)),
                     spec((1, hq, 1, Skv), lambda b, g, i: (b, g, 0, 0))]
        args += [ct, cs]
    if mode == "bias":
        in_specs.append(spec((1, qb, Skv), lambda b, g, i: (b, i, 0)))
        args.append(bias)
    assert dv == LANES
    stat = pltpu.VMEM((hq, qb, LANES), F32)
    kern = functools.partial(_flash_kernel, hq=hq, hk=hk, dk=dk, dv=dv, qb=qb, kb=kb, mode=mode,
                             fox=fox is not None, q_axis=q_axis, n_kv_blocks=Skv // kb)
    return pl.pallas_call(
        kern,
        grid=grid,
        in_specs=in_specs,
        out_specs=spec((1, qb, hq * dv), lambda b, g, i: (b, i, g)),
        out_shape=jax.ShapeDtypeStruct((B, Sq, n_heads * dv), BF16),
        scratch_shapes=[stat, stat, pltpu.VMEM((hq, qb, dv), F32), pltpu.VMEM((hq, qb, kb), BF16), stat]
        + ([stat] if fox is not None else []),
        compiler_params=_params(("parallel", "parallel", "arbitrary")),
        name="flash_" + mode,
    )(*args)


BAND_QB = 2 * CHUNK
BAND_KBLOCKS = (LEFT_CHUNKS * CHUNK) // BAND_QB + 1
BAND_W = BAND_KBLOCKS * BAND_QB


def band_bias_tiles(rel_bias):
    pad = LEFT_CHUNKS * CHUNK
    period = 1024
    m = np.arange(period)
    j_minus_i = np.where(m < period - BAND_QB, m, m - period)
    r = rel_bias[:, np.clip(pad - j_minus_i, -REL_CLIP, REL_CLIP) + REL_CLIP].astype(F32)
    skew = jnp.tile(r, (1, BAND_QB))[:, :BAND_QB * (period - 1)].reshape(-1, BAND_QB, period - 1)
    i = np.arange(BAND_QB)[:, None]
    j = np.arange(BAND_W)[None, :]
    chunk_diff = i // CHUNK + LEFT_CHUNKS - j // CHUNK
    in_band = (chunk_diff >= 0) & (chunk_diff <= LEFT_CHUNKS)
    return jnp.where(in_band[None], skew[:, :, :BAND_W], MASK_NEG)


def _band_kernel(q_ref, k_ref, v_ref, bias_ref, o_ref, *, hb, scale):
    qi = pl.program_id(2)
    for h in range(hb):
        lanes = slice(h * HEAD_DIM, (h + 1) * HEAD_DIM)
        q = q_ref[0, :, lanes]
        s_parts, v_parts = [], []
        for jb in range(BAND_KBLOCKS):
            kblk = qi - (BAND_KBLOCKS - 1) + jb
            ks = pl.multiple_of(jnp.maximum(kblk, 0) * BAND_QB, BAND_QB)
            k = k_ref[0, pl.ds(ks, BAND_QB), lanes]
            v_parts.append(v_ref[0, pl.ds(ks, BAND_QB), lanes])
            s = lax.dot_general(q, k, (((1,), (1,)), ((), ())), preferred_element_type=F32) * scale
            s = s + bias_ref[h, :, jb * BAND_QB:(jb + 1) * BAND_QB]
            s_parts.append(jnp.where(kblk >= 0, s, MASK_NEG))
        s = jnp.concatenate(s_parts, axis=1)
        p = jnp.exp(s - jnp.max(s, axis=1, keepdims=True))
        l = jnp.sum(p, axis=1, keepdims=True)
        acc = jnp.zeros((BAND_QB, HEAD_DIM), F32)
        for jb in range(BAND_KBLOCKS):
            pj = p[:, jb * BAND_QB:(jb + 1) * BAND_QB].astype(v_parts[jb].dtype)
            acc = acc + jnp.dot(pj, v_parts[jb], preferred_element_type=F32)
        o_ref[0, :, lanes] = (acc / l).astype(o_ref.dtype)


def band_attention(qkv, bias_tiles, hb=4):
    B, S, _ = qkv.shape
    H = A_HEADS
    ng = H // hb
    return pl.pallas_call(
        functools.partial(_band_kernel, hb=hb, scale=HEAD_DIM ** -0.5),
        grid=(B, ng, S // BAND_QB),
        in_specs=[pl.BlockSpec((1, BAND_QB, hb * HEAD_DIM), lambda b, g, i: (b, i, g)),
                  pl.BlockSpec((1, S, hb * HEAD_DIM), lambda b, g, i: (b, 0, ng + g)),
                  pl.BlockSpec((1, S, hb * HEAD_DIM), lambda b, g, i: (b, 0, 2 * ng + g)),
                  pl.BlockSpec((hb, BAND_QB, BAND_W), lambda b, g, i: (g, 0, 0))],
        out_specs=pl.BlockSpec((1, BAND_QB, hb * HEAD_DIM), lambda b, g, i: (b, i, g)),
        out_shape=jax.ShapeDtypeStruct((B, S, H * HEAD_DIM), BF16),
        compiler_params=_params(("parallel", "parallel", "arbitrary")),
        name="band_attention",
    )(qkv, qkv, qkv, bias_tiles)


def _fox_cumsum_kernel(f_ref, b_ref, o_ref, carry_ref, *, cb):
    @pl.when(pl.program_id(1) == 0)
    def _():
        carry_ref[...] = jnp.zeros(carry_ref.shape, F32)

    x = f_ref[0].astype(F32) + b_ref[...]
    log_f = jnp.minimum(x, 0.0) - jnp.log1p(jnp.exp(-jnp.abs(x)))
    r = lax.broadcasted_iota(jnp.int32, (cb, cb), 0)
    c = lax.broadcasted_iota(jnp.int32, (cb, cb), 1)
    tri = jnp.where(c <= r, 1.0, 0.0).astype(F32)
    cum = jnp.dot(tri, log_f, preferred_element_type=F32, precision=lax.Precision.HIGHEST) + carry_ref[...]
    o_ref[0] = cum
    carry_ref[...] = cum[cb - 1:cb, :]


def fox_cumsum(proj, col_block, bias_row, batch, seq, cb=256):
    x = proj.reshape(batch, seq, proj.shape[-1])
    return pl.pallas_call(
        functools.partial(_fox_cumsum_kernel, cb=cb),
        grid=(batch, seq // cb),
        in_specs=[pl.BlockSpec((1, cb, LANES), lambda b, i: (b, i, col_block)),
                  pl.BlockSpec((1, LANES), lambda b, i: (0, 0))],
        out_specs=pl.BlockSpec((1, cb, LANES), lambda b, i: (b, i, 0)),
        out_shape=jax.ShapeDtypeStruct((batch, seq, LANES), F32),
        scratch_shapes=[pltpu.VMEM((1, LANES), F32)],
        compiler_params=_params(("parallel", "arbitrary")),
        name="fox_cumsum",
    )(x, bias_row)


IDX_QB = 128
IDX_KC = 512


def _sortable_key(x):
    bits = pltpu.bitcast(x, jnp.int32)
    return bits ^ ((bits >> 31) & jnp.int32(0x7FFFFFFF))


def _indexer_kernel(ki_ref, qit_ref, w_ref, o_ref, key_ref, *, seq, topk):
    qi = pl.program_id(1)
    q0 = qi * IDX_QB
    n_chunks = (q0 + IDX_QB + IDX_KC - 1) // IDX_KC
    qchunk = (q0 + lax.broadcasted_iota(jnp.int32, (1, IDX_QB), 1)) // CHUNK
    int_min = jnp.int32(-2 ** 31)
    w_scale = IDX_HEADS ** -0.5 * IDX_HD ** -0.5

    def score_chunk(c, carry):
        ks = pl.multiple_of(c * IDX_KC, IDX_KC)
        ki = ki_ref[0, pl.ds(ks, IDX_KC), :][:, :IDX_HD]
        acc = jnp.zeros((IDX_KC, IDX_QB), F32)
        for hp in range(IDX_HEADS // 2):
            t = jnp.dot(ki, qit_ref[0, 0, :, hp * 2 * IDX_QB:(hp + 1) * 2 * IDX_QB], preferred_element_type=F32)
            w = w_ref[0, 0, :, hp * 2 * IDX_QB:(hp + 1) * 2 * IDX_QB] * w_scale
            t = jnp.maximum(t, 0.0) * w
            acc = acc + t[:, :IDX_QB] + t[:, IDX_QB:]
        kchunk = (ks + lax.broadcasted_iota(jnp.int32, (IDX_KC, 1), 0)) // CHUNK
        acc = jnp.where(kchunk <= qchunk, acc, -jnp.inf)
        key_ref[pl.ds(ks, IDX_KC), :] = _sortable_key(acc)
        return carry

    lax.fori_loop(0, n_chunks, score_chunk, 0)

    def count_ge(cand):
        def body(c, acc):
            ks = pl.multiple_of(c * IDX_KC, IDX_KC)
            blk = key_ref[pl.ds(ks, IDX_KC), :]
            hit = jnp.where(blk >= cand, 1, 0).astype(jnp.int32)
            return acc + jnp.sum(hit.reshape(IDX_KC // 8, 8, IDX_QB), axis=0)
        acc = lax.fori_loop(0, n_chunks, body, jnp.zeros((8, IDX_QB), jnp.int32))
        return jnp.sum(acc, axis=0, keepdims=True)

    def bit_step(i, ans):
        bit = lax.shift_left(jnp.int32(1), 31 - i)
        cand = ans | bit
        cnt = count_ge(cand ^ int_min)
        return jnp.where(cnt >= topk, cand, ans)

    ans = lax.fori_loop(0, 32, bit_step, jnp.zeros((1, IDX_QB), jnp.int32))
    thr = ans ^ int_min
    neg_inf_key = _sortable_key(jnp.full((1, IDX_QB), -jnp.inf, F32))

    def emit(c, carry):
        ks = pl.multiple_of(c * IDX_KC, IDX_KC)
        blk = key_ref[pl.ds(ks, IDX_KC), :]
        sel = (blk >= thr) & (blk > neg_inf_key)
        bias_t = jnp.where(sel, 0.0, MASK_NEG).astype(F32)
        o_ref[0, :, pl.ds(ks, IDX_KC)] = bias_t.T.astype(o_ref.dtype)
        return carry

    lax.fori_loop(0, n_chunks, emit, 0)

    def fill(c, carry):
        ks = pl.multiple_of(c * IDX_KC, IDX_KC)
        o_ref[0, :, pl.ds(ks, IDX_KC)] = jnp.full((IDX_QB, IDX_KC), MASK_NEG, o_ref.dtype)
        return carry

    lax.fori_loop(n_chunks, seq // IDX_KC, fill, 0)


def dsa_selection_bias(ki, qit, wt, topk):
    B, S, _ = ki.shape
    nq = S // IDX_QB
    return pl.pallas_call(
        functools.partial(_indexer_kernel, seq=S, topk=topk),
        grid=(B, nq),
        in_specs=[pl.BlockSpec((1, S, LANES), lambda b, i: (b, 0, 0)),
                  pl.BlockSpec((1, 1, IDX_HD, IDX_HEADS * IDX_QB), lambda b, i: (b, i, 0, 0)),
                  pl.BlockSpec((1, 1, 1, IDX_HEADS * IDX_QB), lambda b, i: (b, i, 0, 0))],
        out_specs=pl.BlockSpec((1, IDX_QB, S), lambda b, i: (b, i, 0)),
        out_shape=jax.ShapeDtypeStruct((B, S, S), BF16),
        scratch_shapes=[pltpu.VMEM((S, IDX_QB), jnp.int32)],
        compiler_params=_params(("parallel", "arbitrary")),
        name="dsa_indexer",
    )(ki, qit, wt)


PEER_TB_ROUTE = 128
STAT_ROWS = 8


def _peer_route_kernel(qt_ref, sk_ref, s_ref, st_ref, top_ref):
    half = PEER_DKEY // 2
    tb = PEER_TB_ROUTE
    for hp in range(2 * PEER_HEADS):
        q = qt_ref[hp * half:(hp + 1) * half, :].astype(BF16)
        s_ref[hp] = jnp.dot(sk_ref[hp], q, preferred_element_type=F32)

    def head_stats(h, carry):
        def extract_pair(i, c):
            xa, xb = c
            ma = jnp.max(xa, axis=0, keepdims=True)
            mb = jnp.max(xb, axis=0, keepdims=True)
            top_ref[0, pl.ds(i, 1), :] = ma
            top_ref[1, pl.ds(i, 1), :] = mb
            return jnp.where(xa == ma, -jnp.inf, xa), jnp.where(xb == mb, -jnp.inf, xb)

        lax.fori_loop(0, PEER_TOPK, extract_pair, (s_ref[2 * h], s_ref[2 * h + 1]))
        a = top_ref[0]
        b = top_ref[1]
        cand = jnp.concatenate([a[0:1, :] + b] + [a[i:i + 1, :] + b[0:8, :] for i in range(1, 8)]
                               + [a[8:PEER_TOPK, :] + b[0:1, :]], axis=0)

        def extract(i, c):
            x, mx, z, _ = c
            m = jnp.max(x, axis=0, keepdims=True)
            mx = jnp.where(i == 0, m, mx)
            z = z + jnp.exp(m - mx)
            return jnp.where(x == m, -jnp.inf, x), mx, z, m

        zero = jnp.zeros((1, tb), F32)
        _, mx, z, last = lax.fori_loop(0, PEER_TOPK, extract, (cand, zero, zero, zero))
        st_ref[h, 0:1, :] = last
        st_ref[h, 1:2, :] = a[0:1, :]
        st_ref[h, 2:3, :] = b[0:1, :]
        st_ref[h, 3:4, :] = 1.0 / z
        st_ref[h, 4:STAT_ROWS, :] = jnp.zeros((STAT_ROWS - 4, tb), F32)
        return carry

    lax.fori_loop(0, PEER_HEADS, head_stats, 0)


def peer_route(qt, sub_keys):
    R, T = qt.shape
    tb = PEER_TB_ROUTE
    nsub = 2 * PEER_HEADS
    return pl.pallas_call(
        _peer_route_kernel,
        grid=(T // tb,),
        in_specs=[pl.BlockSpec((R, tb), lambda i: (0, i)),
                  pl.BlockSpec((nsub, N_KEYS, PEER_DKEY // 2), lambda i: (0, 0, 0))],
        out_specs=[pl.BlockSpec((nsub, N_KEYS, tb), lambda i: (0, 0, i)),
                   pl.BlockSpec((PEER_HEADS, STAT_ROWS, tb), lambda i: (0, 0, i))],
        out_shape=[jax.ShapeDtypeStruct((nsub, N_KEYS, T), F32),
                   jax.ShapeDtypeStruct((PEER_HEADS, STAT_ROWS, T), F32)],
        scratch_shapes=[pltpu.VMEM((2, PEER_TOPK, tb), F32)],
        compiler_params=_params(("parallel",)),
        name="peer_route",
    )(qt, sub_keys)


PEER_TB = 512
PEER_EB = 512


def _gelu_exact(x):
    return 0.5 * x * (1.0 + lax.erf(x * (2.0 ** -0.5)))


def _peer_dense_kernel(xt_ref, u_ref, v_ref, s_ref, st_ref, y_ref, e2_ref, ht_ref):
    e = pl.program_id(1)

    @pl.when(e == 0)
    def _():
        y_ref[...] = jnp.zeros(y_ref.shape, F32)
        for h in range(PEER_HEADS):
            e2_ref[h] = jnp.exp(s_ref[2 * h + 1] - st_ref[h, 2:3, :])

    act = _gelu_exact(jnp.dot(u_ref[...], xt_ref[...], preferred_element_type=F32))
    rows_per_step = PEER_EB // N_KEYS
    for r in range(rows_per_step):
        i1 = e * rows_per_step + r
        gate = jnp.zeros((N_KEYS, PEER_TB), F32)
        for h in range(PEER_HEADS):
            s1 = s_ref[2 * h, pl.ds(i1, 1), :]
            e1 = jnp.exp(s1 - st_ref[h, 1:2, :]) * st_ref[h, 3:4, :]
            pair = s_ref[2 * h + 1] + s1
            gate = gate + jnp.where(pair >= st_ref[h, 0:1, :], e2_ref[h] * e1, 0.0)
        ht_ref[r * N_KEYS:(r + 1) * N_KEYS, :] = (gate * act[r * N_KEYS:(r + 1) * N_KEYS, :]).astype(BF16)
    y_ref[...] += lax.dot_general(ht_ref[...], v_ref[...], (((0,), (0,)), ((), ())), preferred_element_type=F32)


def peer_dense(xt, u, v, s_t, stats):
    D, T = xt.shape
    E = u.shape[0]
    nsub = 2 * PEER_HEADS
    once = pl.Buffered(1)
    return pl.pallas_call(
        _peer_dense_kernel,
        grid=(T // PEER_TB, E // PEER_EB),
        in_specs=[pl.BlockSpec((D, PEER_TB), lambda i, e: (0, i), pipeline_mode=once),
                  pl.BlockSpec((PEER_EB, D), lambda i, e: (e, 0)),
                  pl.BlockSpec((PEER_EB, D), lambda i, e: (e, 0)),
                  pl.BlockSpec((nsub, N_KEYS, PEER_TB), lambda i, e: (0, 0, i), pipeline_mode=once),
                  pl.BlockSpec((PEER_HEADS, STAT_ROWS, PEER_TB), lambda i, e: (0, 0, i))],
        out_specs=pl.BlockSpec((PEER_TB, D), lambda i, e: (i, 0)),
        out_shape=jax.ShapeDtypeStruct((T, D), F32),
        scratch_shapes=[pltpu.VMEM((PEER_HEADS, N_KEYS, PEER_TB), F32), pltpu.VMEM((PEER_EB, PEER_TB), BF16)],
        compiler_params=_params(("parallel", "arbitrary")),
        name="peer_dense",
    )(xt, u, v, s_t, stats)


def _pad_cols(w, width):
    return jnp.pad(w, ((0, 0), (0, width - w.shape[1])))


def mixer_ab(hb, batch, seq, w_in, rel_bias, q_norm, w_uq, kv_norm, w_ukv, w_out):
    T, D = hb.shape
    a_w = A_HEADS * HEAD_DIM
    w_qkv = w_in[:, :3 * a_w].astype(BF16)
    c_cols = MLA_Q_LORA + MLA_KV_LORA + LANES
    w_c = _pad_cols(w_in[:, 3 * a_w:], _round_up(c_cols, 512)).astype(BF16)
    qkv = matmul(hb, w_qkv, BF16)
    cproj = matmul(hb, w_c, F32)

    oa = band_attention(qkv.reshape(batch, seq, 3 * a_w), band_bias_tiles(rel_bias))

    qh = MLA_NOPE + MLA_ROPE
    w_uq_p = jnp.pad(w_uq.reshape(MLA_Q_LORA, B_HEADS, qh), ((0, 0), (0, 0), (0, 2 * LANES - qh)))
    w_uq_p = w_uq_p.reshape(MLA_Q_LORA, B_HEADS * 2 * LANES).astype(BF16)
    cq_n = rms_norm_cols(cproj, 0, MLA_Q_LORA, q_norm)
    ckv_n = rms_norm_cols(cproj, MLA_Q_LORA // MLA_KV_LORA, MLA_KV_LORA, kv_norm)
    q_lat = matmul(cq_n, w_uq_p, F32)
    q_scale = (MLA_NOPE + MLA_ROPE) ** -0.5 * LOG2E
    q_cat = rope_cols(q_lat, 0, B_HEADS, 2 * LANES, MLA_ROPE // 2,
                      rope_tables(seq, MLA_ROPE, 2 * LANES, offset=MLA_NOPE, scale=q_scale), seq)
    kv = matmul(ckv_n, w_ukv.astype(BF16), BF16)
    k_cat, v_cat = mla_assemble_kv(kv, cproj, (MLA_Q_LORA + MLA_KV_LORA) // LANES,
                                   rope_tables(seq, MLA_ROPE, LANES), seq)
    ob = flash_attention(q_cat.reshape(batch, seq, -1), k_cat.reshape(batch, seq, -1), v_cat.reshape(batch, seq, -1),
                         n_heads=B_HEADS, hq=2, hk=2, dk=2 * LANES, dv=MLA_V, q_blk=0, k_blk=0, v_blk=0,
                         mode="chunk")
    return matmul_cat(oa.reshape(T, -1), ob.reshape(T, -1), w_out.astype(BF16), F32)


def mixer_cd(hb, batch, seq, w_in, forget_bias, w_out):
    T, D = hb.shape
    c_qw, c_kw, d_w = C_HEADS * HEAD_DIM, C_KV_HEADS * HEAD_DIM, D_HEADS * HEAD_DIM
    i_w = IDX_HEADS * IDX_HD
    offs = np.cumsum([0, c_qw, c_kw, c_kw, i_w, IDX_HD, IDX_HEADS, d_w, d_w, d_w, D_HEADS])
    col = lambda n: w_in[:, offs[n]:offs[n + 1]]
    q_scale = HEAD_DIM ** -0.5 * LOG2E
    w_a = jnp.concatenate([col(0) * q_scale, col(1), col(2), col(6) * q_scale, col(7), col(8)], axis=1).astype(BF16)
    w_b = jnp.concatenate([col(3), _pad_cols(col(4), LANES), _pad_cols(col(5), LANES), _pad_cols(col(9), LANES)], axis=1)
    w_b = _pad_cols(w_b, _round_up(w_b.shape[1], 512)).astype(BF16)
    proj_a = matmul(hb, w_a, BF16)
    proj_b = matmul(hb, w_b, F32)
    qk_w = c_qw + c_kw
    blk_ki, blk_wi, blk_fd = i_w // LANES, i_w // LANES + 1, i_w // LANES + 2

    qk_rot = rope_cols(proj_a, 0, qk_w // LANES, LANES, HEAD_DIM // 2, rope_tables(seq, HEAD_DIM, LANES), seq)
    qi_rot = rope_cols(proj_b, 0, i_w // LANES, LANES, IDX_HD // 2, rope_tables(seq, IDX_HD, LANES, reps=2), seq)
    ki_rot = rope_cols(proj_b, blk_ki, 1, LANES, IDX_HD // 2, rope_tables(seq, IDX_HD, LANES), seq)
    nq = seq // IDX_QB
    qit = qi_rot.reshape(batch, nq, IDX_QB, IDX_HEADS, IDX_HD).transpose(0, 1, 4, 3, 2)
    qit = qit.reshape(batch, nq, IDX_HD, IDX_HEADS * IDX_QB)
    wi = proj_b[:, blk_wi * LANES:blk_wi * LANES + IDX_HEADS]
    wt = wi.reshape(batch, nq, IDX_QB, IDX_HEADS).transpose(0, 1, 3, 2).reshape(batch, nq, 1, IDX_HEADS * IDX_QB)
    sel_bias = dsa_selection_bias(ki_rot.reshape(batch, seq, LANES), qit, wt, min(DSA_TOPK_MAX, seq // 4))
    qk3 = qk_rot.reshape(batch, seq, qk_w)
    pa3 = proj_a.reshape(batch, seq, -1)
    rep = C_HEADS // C_KV_HEADS
    oc = flash_attention(qk3, qk3, pa3, n_heads=C_HEADS, hq=rep, hk=1, dk=HEAD_DIM, dv=HEAD_DIM,
                         q_blk=0, k_blk=c_qw // HEAD_DIM, v_blk=qk_w // HEAD_DIM, mode="bias", bias=sel_bias)

    fbias = _pad_cols(forget_bias.reshape(1, D_HEADS).astype(F32), LANES)
    cum = fox_cumsum(proj_b, blk_fd, fbias, batch, seq)[:, :, :D_HEADS]
    cum_t = cum.transpose(0, 2, 1)
    hd = 2
    d0 = (qk_w + c_kw) // (hd * HEAD_DIM)
    od = flash_attention(pa3, pa3, pa3, n_heads=D_HEADS, hq=hd, hk=hd, dk=HEAD_DIM, dv=HEAD_DIM,
                         q_blk=d0, k_blk=d0 + d_w // (hd * HEAD_DIM), v_blk=d0 + 2 * d_w // (hd * HEAD_DIM),
                         mode="causal", fox=(cum_t[..., None], cum_t[:, :, None, :]))
    return matmul_cat(oc.reshape(T, -1), od.reshape(T, -1), w_out.astype(BF16), F32)


def memory_cross_attention(hb, mem_b, batch, seq, w_q, w_kv, w_o):
    T, D = hb.shape
    mem_w = MEM_HEADS * MEM_HD
    q_scale = MEM_HD ** -0.5 * LOG2E
    q = matmul(hb, (w_q * q_scale).astype(BF16), BF16)
    kv = matmul(mem_b.reshape(-1, D), w_kv.astype(BF16), BF16)
    kv3 = kv.reshape(batch, -1, 2 * mem_w)
    o = flash_attention(q.reshape(batch, seq, mem_w), kv3, kv3, n_heads=MEM_HEADS, hq=MEM_HEADS, hk=MEM_HEADS,
                        dk=MEM_HD, dv=MEM_HD, q_blk=0, k_blk=0, v_blk=1, mode="none", kb=kv3.shape[1])
    return matmul(o.reshape(T, mem_w), w_o.astype(BF16), F32)


def peer_ffn(hb, w_q, sub_keys, u_tabs, v_tabs, layer):
    ht = hb.T
    qt = matmul(w_q.T.astype(BF16), ht, F32)
    sk = sub_keys.reshape(2 * PEER_HEADS, N_KEYS, PEER_DKEY // 2).astype(BF16)
    s_t, stats = peer_route(qt, sk)
    return peer_dense(ht, cast_layer(u_tabs, layer), cast_layer(v_tabs, layer), s_t, stats)


def kernel(x, mem, ab_w_in, a_rel_bias, b_q_norm, b_w_uq, b_kv_norm, b_w_ukv, ab_w_out, cd_w_in, d_forget_bias,
           cd_w_out, mem_w_q, mem_w_kv, mem_w_o, peer_w_q, peer_sub_keys, peer_u, peer_v, ln_g, ln_b):
    batch, seq, d_model = x.shape
    h = x.reshape(batch * seq, d_model)
    hb = h.astype(BF16)
    mem_b = mem.astype(BF16)
    for layer in range(DEPTH):
        j = layer // 2
        if layer % 2 == 0:
            y = mixer_ab(hb, batch, seq, ab_w_in[j], a_rel_bias[j], b_q_norm[j], b_w_uq[j], b_kv_norm[j],
                         b_w_ukv[j], ab_w_out[j])
        else:
            y = mixer_cd(hb, batch, seq, cd_w_in[j], d_forget_bias[j], cd_w_out[j])
        h, hb = deepnorm_ln(h, y, ln_g[layer, 0], ln_b[layer, 0])
        y = memory_cross_attention(hb, mem_b, batch, seq, mem_w_q[layer], mem_w_kv[layer], mem_w_o[layer])
        h, hb = deepnorm_ln(h, y, ln_g[layer, 1], ln_b[layer, 1])
        y = peer_ffn(hb, peer_w_q[layer], peer_sub_keys[layer], peer_u, peer_v, layer)
        h, hb = deepnorm_ln(h, y, ln_g[layer, 2], ln_b[layer, 2])
    return h.reshape(batch, seq, d_model)
```

```python
import functools

import jax
import jax.numpy as jnp
import numpy as np
from jax import lax
from jax.experimental import pallas as pl
from jax.experimental.pallas import tpu as pltpu

F32 = jnp.float32
BF16 = jnp.bfloat16

CHUNK = 64
HEAD_DIM = 128
A_HEADS = 16
LEFT_CHUNKS = 8
REL_CLIP = 256
B_HEADS = 16
MLA_Q_LORA = 1536
MLA_KV_LORA = 512
MLA_NOPE = 128
MLA_ROPE = 64
MLA_V = 128
C_HEADS = 16
C_KV_HEADS = 4
IDX_HEADS = 32
IDX_HD = 64
DSA_TOPK_MAX = 256
D_HEADS = 16
MEM_HEADS = 4
MEM_HD = 128
PEER_HEADS = 8
PEER_DKEY = 256
N_KEYS = 128
PEER_TOPK = 16
ROPE_THETA = 10000.0
LN_EPS = 1e-5
RMS_EPS = 1e-6
DEPTH = 2
DEEPNORM_ALPHA = (2 * DEPTH) ** 0.25

LANES = 128
V7X_VMEM_LIMIT = 56 * 1024 * 1024
MASK_NEG = -1e30
LOG2E = 1.4426950408889634


def _params(sem, vmem=V7X_VMEM_LIMIT, flags=None):
    return pltpu.CompilerParams(dimension_semantics=sem, vmem_limit_bytes=vmem, flags=flags)


def _round_up(n, m):
    return (n + m - 1) // m * m


def _mm_kernel(a_ref, w_ref, o_ref):
    o_ref[...] = jnp.dot(a_ref[...], w_ref[...], preferred_element_type=F32).astype(o_ref.dtype)


def matmul(a, w, out_dtype, tm=1024, tn=512):
    M, K = a.shape
    N = w.shape[1]
    tm = min(tm, M)
    tn = min(tn, N)
    assert M % tm == 0 and N % tn == 0, (M, N, tm, tn)
    return pl.pallas_call(
        _mm_kernel,
        grid=(M // tm, N // tn),
        in_specs=[pl.BlockSpec((tm, K), lambda i, j: (i, 0)),
                  pl.BlockSpec((K, tn), lambda i, j: (0, j))],
        out_specs=pl.BlockSpec((tm, tn), lambda i, j: (i, j)),
        out_shape=jax.ShapeDtypeStruct((M, N), out_dtype),
        compiler_params=_params(("parallel", "arbitrary")),
        name="matmul",
    )(a, w)


def _mm2_kernel(a1_ref, a2_ref, w_ref, o_ref):
    a = jnp.concatenate([a1_ref[...], a2_ref[...]], axis=1)
    o_ref[...] = jnp.dot(a, w_ref[...], preferred_element_type=F32).astype(o_ref.dtype)


def matmul_cat(a1, a2, w, out_dtype, tm=1024, tn=512):
    M, K1 = a1.shape
    K2 = a2.shape[1]
    N = w.shape[1]
    tm = min(tm, M)
    tn = min(tn, N)
    assert M % tm == 0 and N % tn == 0 and w.shape[0] == K1 + K2
    return pl.pallas_call(
        _mm2_kernel,
        grid=(M // tm, N // tn),
        in_specs=[pl.BlockSpec((tm, K1), lambda i, j: (i, 0)),
                  pl.BlockSpec((tm, K2), lambda i, j: (i, 0)),
                  pl.BlockSpec((K1 + K2, tn), lambda i, j: (0, j))],
        out_specs=pl.BlockSpec((tm, tn), lambda i, j: (i, j)),
        out_shape=jax.ShapeDtypeStruct((M, N), out_dtype),
        compiler_params=_params(("parallel", "arbitrary")),
        name="matmul_cat",
    )(a1, a2, w)


def _cast_kernel(w_ref, o_ref):
    o_ref[...] = w_ref[...].astype(o_ref.dtype)


def _cast_t_kernel(w_ref, o_ref):
    o_ref[...] = w_ref[...].T.astype(o_ref.dtype)


def cast_layer(w, layer, transpose=False, tr=512):
    _, R, C = w.shape
    out_spec = pl.BlockSpec((C, tr), lambda i: (0, i)) if transpose else pl.BlockSpec((tr, C), lambda i: (i, 0))
    return pl.pallas_call(
        _cast_t_kernel if transpose else _cast_kernel,
        grid=(R // tr,),
        in_specs=[pl.BlockSpec((None, tr, C), lambda i: (layer, i, 0))],
        out_specs=out_spec,
        out_shape=jax.ShapeDtypeStruct((C, R) if transpose else (R, C), BF16),
        compiler_params=_params(("parallel",)),
        name="cast_layer_t" if transpose else "cast_layer",
    )(w)


def _ln_kernel(h_ref, y_ref, g_ref, b_ref, o_ref, ob_ref):
    z = DEEPNORM_ALPHA * h_ref[...] + y_ref[...]
    zc = z - jnp.mean(z, axis=-1, keepdims=True)
    var = jnp.mean(zc * zc, axis=-1, keepdims=True)
    out = zc * lax.rsqrt(var + LN_EPS) * g_ref[...] + b_ref[...]
    o_ref[...] = out
    ob_ref[...] = out.astype(BF16)


def deepnorm_ln(h, y, g, b, tb=256):
    T, D = h.shape
    row = pl.BlockSpec((tb, D), lambda i: (i, 0))
    vec = pl.BlockSpec((1, D), lambda i: (0, 0))
    return pl.pallas_call(
        _ln_kernel,
        grid=(T // tb,),
        in_specs=[row, row, vec, vec],
        out_specs=[row, row],
        out_shape=[jax.ShapeDtypeStruct((T, D), F32), jax.ShapeDtypeStruct((T, D), BF16)],
        compiler_params=_params(("parallel",)),
        name="deepnorm_ln",
    )(h, y, g.reshape(1, D), b.reshape(1, D))


def _rms_kernel(x_ref, g_ref, o_ref):
    x = x_ref[...].astype(F32)
    o_ref[...] = (x * lax.rsqrt(jnp.mean(x * x, axis=-1, keepdims=True) + RMS_EPS) * g_ref[...]).astype(o_ref.dtype)


def rms_norm_cols(x, col_block, width, g, tb=512):
    T = x.shape[0]
    return pl.pallas_call(
        _rms_kernel,
        grid=(T // tb,),
        in_specs=[pl.BlockSpec((tb, width), lambda i: (i, col_block)),
                  pl.BlockSpec((1, width), lambda i: (0, 0))],
        out_specs=pl.BlockSpec((tb, width), lambda i: (i, 0)),
        out_shape=jax.ShapeDtypeStruct((T, width), BF16),
        compiler_params=_params(("parallel",)),
        name="rms_norm",
    )(x, g.reshape(1, width))


def rope_tables(seq, dim, group, offset=0, reps=1, scale=1.0):
    half = dim // 2
    inv = ROPE_THETA ** (-jnp.arange(half, dtype=F32) * 2.0 / dim)
    ang = jnp.arange(seq, dtype=F32)[:, None] * inv[None, :]
    cos, sin = jnp.cos(ang), jnp.sin(ang)
    zeros_tail = jnp.zeros((seq, group - offset - dim * reps), F32)
    head = jnp.ones((seq, offset), F32)
    zhead = jnp.zeros((seq, offset), F32)
    zhalf = jnp.zeros((seq, half), F32)
    c = jnp.concatenate([head] + [cos, cos] * reps + [zeros_tail], axis=1)
    sa = jnp.concatenate([zhead] + [-sin, zhalf] * reps + [zeros_tail], axis=1)
    sb = jnp.concatenate([zhead] + [zhalf, sin] * reps + [zeros_tail], axis=1)
    return c * scale, sa * scale, sb * scale


def _rope_kernel(x_ref, c_ref, sa_ref, sb_ref, o_ref, *, groups, group, half):
    c, sa, sb = c_ref[...], sa_ref[...], sb_ref[...]
    for g in range(groups):
        x = x_ref[:, g * group:(g + 1) * group].astype(F32)
        out = x * c + pltpu.roll(x, group - half, 1) * sa + pltpu.roll(x, half, 1) * sb
        o_ref[:, g * group:(g + 1) * group] = out.astype(o_ref.dtype)


def rope_cols(x, col_block, groups, group, half, tables, seq, tb=256):
    T = x.shape[0]
    W = groups * group
    nsb = seq // tb
    tab = pl.BlockSpec((tb, group), lambda i: (i % nsb, 0))
    return pl.pallas_call(
        functools.partial(_rope_kernel, groups=groups, group=group, half=half),
        grid=(T // tb,),
        in_specs=[pl.BlockSpec((tb, W), lambda i: (i, col_block)), tab, tab, tab],
        out_specs=pl.BlockSpec((tb, W), lambda i: (i, 0)),
        out_shape=jax.ShapeDtypeStruct((T, W), BF16),
        compiler_params=_params(("parallel",)),
        name="rope",
    )(x, *tables)


def _mla_kv_kernel(kv_ref, kr_ref, c_ref, sa_ref, sb_ref, k_ref, v_ref, *, heads, half):
    kr = kr_ref[...].astype(F32)
    kr = kr * c_ref[...] + pltpu.roll(kr, LANES - half, 1) * sa_ref[...] + pltpu.roll(kr, half, 1) * sb_ref[...]
    kr = kr.astype(k_ref.dtype)
    for h in range(heads):
        k_ref[:, 2 * h * LANES:(2 * h + 1) * LANES] = kv_ref[:, 2 * h * LANES:(2 * h + 1) * LANES]
        k_ref[:, (2 * h + 1) * LANES:(2 * h + 2) * LANES] = kr
        v_ref[:, h * LANES:(h + 1) * LANES] = kv_ref[:, (2 * h + 1) * LANES:(2 * h + 2) * LANES]


def mla_assemble_kv(kv, cproj, kr_col_block, tables, seq, tb=256):
    T, W = kv.shape
    nsb = seq // tb
    tab = pl.BlockSpec((tb, LANES), lambda i: (i % nsb, 0))
    return pl.pallas_call(
        functools.partial(_mla_kv_kernel, heads=B_HEADS, half=MLA_ROPE // 2),
        grid=(T // tb,),
        in_specs=[pl.BlockSpec((tb, W), lambda i: (i, 0)),
                  pl.BlockSpec((tb, LANES), lambda i: (i, kr_col_block)), tab, tab, tab],
        out_specs=[pl.BlockSpec((tb, W), lambda i: (i, 0)), pl.BlockSpec((tb, W // 2), lambda i: (i, 0))],
        out_shape=[jax.ShapeDtypeStruct((T, W), BF16), jax.ShapeDtypeStruct((T, W // 2), BF16)],
        compiler_params=_params(("parallel",)),
        name="mla_assemble_kv",
    )(kv, cproj, *tables)


FLASH_RB = 128


def _flash_kernel(*refs, hq, hk, dk, dv, qb, kb, mode, fox, q_axis, n_kv_blocks):
    it = iter(refs)
    q_ref, k_ref, v_ref = next(it), next(it), next(it)
    ct_ref = cs_ref = bias_ref = None
    if fox:
        ct_ref, cs_ref = next(it), next(it)
    if mode == "bias":
        bias_ref = next(it)
    o_ref, m_ref, l_ref, acc_ref, p_ref, alpha_ref = next(it), next(it), next(it), next(it), next(it), next(it)
    ct_lanes_ref = next(it) if fox else None

    qi = pl.program_id(q_axis)
    m_ref[...] = jnp.full(m_ref.shape, MASK_NEG, F32)
    l_ref[...] = jnp.zeros(l_ref.shape, F32)
    acc_ref[...] = jnp.zeros(acc_ref.shape, F32)
    if fox:
        for h in range(hq):
            ct_lanes_ref[h] = jnp.broadcast_to(ct_ref[0, h] * LOG2E, (qb, LANES))

    n_sub = qb // FLASH_RB
    per_kv = kb // FLASH_RB
    rows = lax.broadcasted_iota(jnp.int32, (FLASH_RB, kb), 0)
    cols = lax.broadcasted_iota(jnp.int32, (FLASH_RB, kb), 1)

    def diag_mask(r):
        row_in_kv = rows + (r % per_kv) * FLASH_RB
        if mode == "causal":
            return cols <= row_in_kv
        if mode == "chunk":
            return (cols // CHUNK) <= (row_in_kv // CHUNK)
        return None

    def step(j, visible):
        ks = pl.multiple_of(j * kb, kb)
        for h in range(hq):
            g = h if hk == hq else 0
            k = k_ref[0, pl.ds(ks, kb), g * dk:(g + 1) * dk]
            v = v_ref[0, pl.ds(ks, kb), g * dv:(g + 1) * dv]
            s_all = lax.dot_general(q_ref[0, :, h * dk:(h + 1) * dk], k, (((1,), (1,)), ((), ())),
                                    preferred_element_type=F32)
            cs = cs_ref[0, h, :, pl.ds(ks, kb)] * LOG2E if fox else None
            for r in range(n_sub):
                rs = slice(r * FLASH_RB, (r + 1) * FLASH_RB)
                if visible[r] == "none":
                    p_ref[h, rs, :] = jnp.zeros((FLASH_RB, kb), p_ref.dtype)
                    alpha_ref[h, rs, :] = jnp.ones((FLASH_RB, LANES), F32)
                    continue
                s = s_all[rs]
                if bias_ref is not None:
                    s = s + bias_ref[0, rs, pl.ds(ks, kb)].astype(F32)
                blocks = [s[:, c * LANES:(c + 1) * LANES] for c in range(kb // LANES)]
                if fox:
                    ct = ct_lanes_ref[h, rs]
                    blocks = [blk + ct - cs[:, c * LANES:(c + 1) * LANES] for c, blk in enumerate(blocks)]
                if visible[r] == "diag" and mode in ("causal", "chunk"):
                    mask = diag_mask(r)
                    blocks = [jnp.where(mask[:, c * LANES:(c + 1) * LANES], blk, MASK_NEG)
                              for c, blk in enumerate(blocks)]
                blk_max = blocks[0]
                for blk in blocks[1:]:
                    blk_max = jnp.maximum(blk_max, blk)
                m_prev = m_ref[h, rs]
                m_new = jnp.maximum(m_prev, jnp.max(blk_max, axis=1, keepdims=True))
                alpha = jnp.exp2(m_prev - m_new)
                p_blocks = [jnp.exp2(blk - m_new) for blk in blocks]
                p_lanes = p_blocks[0]
                for pb in p_blocks[1:]:
                    p_lanes = p_lanes + pb
                l_ref[h, rs] = alpha * l_ref[h, rs] + p_lanes
                for c, pb in enumerate(p_blocks):
                    p_ref[h, rs, c * LANES:(c + 1) * LANES] = pb.astype(p_ref.dtype)
                alpha_ref[h, rs, :] = alpha
                m_ref[h, rs] = m_new
            pv = jnp.dot(p_ref[h], v, preferred_element_type=F32)
            acc_ref[h] = alpha_ref[h] * acc_ref[h] + pv

    def body(j, carry):
        step(j, ["full"] * n_sub)
        return carry

    if mode == "none":
        lax.fori_loop(0, n_kv_blocks, body, 0)
    else:
        kv_per_q = qb // kb
        lax.fori_loop(0, qi * kv_per_q, body, 0)
        for t in range(kv_per_q):
            kinds = ["full" if r // per_kv > t else "diag" if r // per_kv == t else "none" for r in range(n_sub)]
            step(qi * kv_per_q + t, kinds)

    for h in range(hq):
        l = jnp.sum(l_ref[h], axis=1, keepdims=True)
        o_ref[0, :, h * dv:(h + 1) * dv] = (acc_ref[h] / l).astype(o_ref.dtype)


def flash_attention(q, k, v, *, n_heads, hq, hk, dk, dv, q_blk, k_blk, v_blk, mode,
                    qb=512, kb=256, fox=None, bias=None):
    B, Sq = q.shape[0], q.shape[1]
    Skv = k.shape[1]
    kb = min(kb, Skv)
    nq, ng = Sq // qb, n_heads // hq
    assert hk in (hq, 1) and qb % kb == 0 and kb % FLASH_RB == 0
    if mode in ("causal", "chunk", "bias"):
        assert Sq == Skv
    if mode != "bias":
        grid = (B, ng, nq)
        q_axis = 2

        def spec(shape, fn):
            return pl.BlockSpec(shape, lambda b, g, i: fn(b, g, i))
    else:
        grid = (B, nq, ng)
        q_axis = 1

        def spec(shape, fn):
            return pl.BlockSpec(shape, lambda b, i, g: fn(b, g, i))

    in_specs = [
        spec((1, qb, hq * dk), lambda b, g, i: (b, i, q_blk + g)),
        spec((1, Skv, hk * dk), lambda b, g, i: (b, 0, k_blk + g)),
        spec((1, Skv, hk * dv), lambda b, g, i: (b, 0, v_blk + g)),
    ]
    args = [q, k, v]
    if fox is not None:
        ct, cs = fox
        in_specs += [spec((1, hq, qb, 1), lambda b, g, i: (b, g, i, 0)),
                     spec((1, hq, 1, Skv), lambda b, g, i: (b, g, 0, 0))]
        args += [ct, cs]
    if mode == "bias":
        in_specs.append(spec((1, qb, Skv), lambda b, g, i: (b, i, 0)))
        args.append(bias)
    assert dv == LANES
    stat = pltpu.VMEM((hq, qb, LANES), F32)
    kern = functools.partial(_flash_kernel, hq=hq, hk=hk, dk=dk, dv=dv, qb=qb, kb=kb, mode=mode,
                             fox=fox is not None, q_axis=q_axis, n_kv_blocks=Skv // kb)
    return pl.pallas_call(
        kern,
        grid=grid,
        in_specs=in_specs,
        out_specs=spec((1, qb, hq * dv), lambda b, g, i: (b, i, g)),
        out_shape=jax.ShapeDtypeStruct((B, Sq, n_heads * dv), BF16),
        scratch_shapes=[stat, stat, pltpu.VMEM((hq, qb, dv), F32), pltpu.VMEM((hq, qb, kb), BF16), stat]
        + ([stat] if fox is not None else []),
        compiler_params=_params(("parallel", "parallel", "arbitrary")),
        name="flash_" + mode,
    )(*args)


BAND_QB = 4 * CHUNK
BAND_KBLOCKS = (LEFT_CHUNKS * CHUNK) // BAND_QB + 1
BAND_W = BAND_KBLOCKS * BAND_QB


def band_bias_tiles(rel_bias):
    pad = LEFT_CHUNKS * CHUNK
    period = BAND_QB + BAND_W
    m = np.arange(period)
    j_minus_i = np.where(m < period - BAND_QB, m, m - period)
    r = rel_bias[:, np.clip(pad - j_minus_i, -REL_CLIP, REL_CLIP) + REL_CLIP].astype(F32)
    skew = jnp.tile(r, (1, BAND_QB))[:, :BAND_QB * (period - 1)].reshape(-1, BAND_QB, period - 1)
    i = np.arange(BAND_QB)[:, None]
    j = np.arange(BAND_W)[None, :]
    chunk_diff = i // CHUNK + LEFT_CHUNKS - j // CHUNK
    in_band = (chunk_diff >= 0) & (chunk_diff <= LEFT_CHUNKS)
    return jnp.where(in_band[None], skew[:, :, :BAND_W] * LOG2E, MASK_NEG)


def _band_kernel(q_ref, k_ref, v_ref, bias_ref, o_ref, s_ref, p_ref, l_ref, *, hb):
    qi = pl.program_id(2)
    for h in range(hb):
        lanes = slice(h * HEAD_DIM, (h + 1) * HEAD_DIM)
        q = q_ref[0, :, lanes]
        v_parts = []
        for jb in range(BAND_KBLOCKS):
            kblk = qi - (BAND_KBLOCKS - 1) + jb
            ks = pl.multiple_of(jnp.maximum(kblk, 0) * BAND_QB, BAND_QB)
            k = k_ref[0, pl.ds(ks, BAND_QB), lanes]
            v_parts.append(v_ref[0, pl.ds(ks, BAND_QB), lanes])
            s = lax.dot_general(q, k, (((1,), (1,)), ((), ())), preferred_element_type=F32)
            s = s + bias_ref[h, :, jb * BAND_QB:(jb + 1) * BAND_QB]
            s_ref[h, :, jb * BAND_QB:(jb + 1) * BAND_QB] = jnp.where(kblk >= 0, s, MASK_NEG)
        for r in range(BAND_QB // CHUNK):
            rs = slice(r * CHUNK, (r + 1) * CHUNK)
            blocks = [s_ref[h, rs, c * LANES:(c + 1) * LANES] for c in range(BAND_W // LANES)]
            blk_max = blocks[0]
            for blk in blocks[1:]:
                blk_max = jnp.maximum(blk_max, blk)
            m = jnp.broadcast_to(jnp.max(blk_max, axis=1, keepdims=True), (CHUNK, LANES))
            p_lanes = jnp.zeros((CHUNK, LANES), F32)
            for c, blk in enumerate(blocks):
                p = jnp.exp2(blk - m)
                p_lanes = p_lanes + p
                p_ref[h, rs, c * LANES:(c + 1) * LANES] = p.astype(p_ref.dtype)
            l_ref[h, rs, :] = p_lanes
        acc = jnp.zeros((BAND_QB, HEAD_DIM), F32)
        for jb in range(BAND_KBLOCKS):
            acc = acc + jnp.dot(p_ref[h, :, jb * BAND_QB:(jb + 1) * BAND_QB], v_parts[jb], preferred_element_type=F32)
        l = jnp.sum(l_ref[h], axis=1, keepdims=True)
        o_ref[0, :, lanes] = (acc / l).astype(o_ref.dtype)


def band_attention(qkv, bias_tiles, hb=4):
    B, S, _ = qkv.shape
    H = A_HEADS
    ng = H // hb
    return pl.pallas_call(
        functools.partial(_band_kernel, hb=hb),
        grid=(B, ng, S // BAND_QB),
        in_specs=[pl.BlockSpec((1, BAND_QB, hb * HEAD_DIM), lambda b, g, i: (b, i, g)),
                  pl.BlockSpec((1, S, hb * HEAD_DIM), lambda b, g, i: (b, 0, ng + g)),
                  pl.BlockSpec((1, S, hb * HEAD_DIM), lambda b, g, i: (b, 0, 2 * ng + g)),
                  pl.BlockSpec((hb, BAND_QB, BAND_W), lambda b, g, i: (g, 0, 0))],
        out_specs=pl.BlockSpec((1, BAND_QB, hb * HEAD_DIM), lambda b, g, i: (b, i, g)),
        out_shape=jax.ShapeDtypeStruct((B, S, H * HEAD_DIM), BF16),
        scratch_shapes=[pltpu.VMEM((hb, BAND_QB, BAND_W), F32), pltpu.VMEM((hb, BAND_QB, BAND_W), BF16),
                        pltpu.VMEM((hb, BAND_QB, LANES), F32)],
        compiler_params=_params(("parallel", "parallel", "arbitrary")),
        name="band_attention",
    )(qkv, qkv, qkv, bias_tiles)


def _fox_cumsum_kernel(f_ref, b_ref, o_ref, carry_ref, *, cb):
    @pl.when(pl.program_id(1) == 0)
    def _():
        carry_ref[...] = jnp.zeros(carry_ref.shape, F32)

    x = f_ref[0].astype(F32) + b_ref[...]
    log_f = jnp.minimum(x, 0.0) - jnp.log1p(jnp.exp(-jnp.abs(x)))
    r = lax.broadcasted_iota(jnp.int32, (cb, cb), 0)
    c = lax.broadcasted_iota(jnp.int32, (cb, cb), 1)
    tri = jnp.where(c <= r, 1.0, 0.0).astype(F32)
    cum = jnp.dot(tri, log_f, preferred_element_type=F32, precision=lax.Precision.HIGHEST) + carry_ref[...]
    o_ref[0] = cum
    carry_ref[...] = cum[cb - 1:cb, :]


def fox_cumsum(proj, col_block, bias_row, batch, seq, cb=256):
    x = proj.reshape(batch, seq, proj.shape[-1])
    return pl.pallas_call(
        functools.partial(_fox_cumsum_kernel, cb=cb),
        grid=(batch, seq // cb),
        in_specs=[pl.BlockSpec((1, cb, LANES), lambda b, i: (b, i, col_block)),
                  pl.BlockSpec((1, LANES), lambda b, i: (0, 0))],
        out_specs=pl.BlockSpec((1, cb, LANES), lambda b, i: (b, i, 0)),
        out_shape=jax.ShapeDtypeStruct((batch, seq, LANES), F32),
        scratch_shapes=[pltpu.VMEM((1, LANES), F32)],
        compiler_params=_params(("parallel", "arbitrary")),
        name="fox_cumsum",
    )(x, bias_row)


IDX_QB = 128
IDX_KC = 512


def _sortable_key(x):
    bits = pltpu.bitcast(x, jnp.int32)
    return bits ^ ((bits >> 31) & jnp.int32(0x7FFFFFFF))


def _indexer_kernel(ki_ref, qit_ref, w_ref, o_ref, key_ref, *, seq, topk):
    qi = pl.program_id(1)
    q0 = qi * IDX_QB
    n_chunks = (q0 + IDX_QB + IDX_KC - 1) // IDX_KC
    qchunk = (q0 + lax.broadcasted_iota(jnp.int32, (1, IDX_QB), 1)) // CHUNK
    int_min = jnp.int32(-2 ** 31)
    w_scale = IDX_HEADS ** -0.5 * IDX_HD ** -0.5

    def score_chunk(c, carry):
        ks = pl.multiple_of(c * IDX_KC, IDX_KC)
        ki = ki_ref[0, pl.ds(ks, IDX_KC), :][:, :IDX_HD]
        acc = jnp.zeros((IDX_KC, IDX_QB), F32)
        for hp in range(IDX_HEADS // 2):
            t = jnp.dot(ki, qit_ref[0, 0, :, hp * 2 * IDX_QB:(hp + 1) * 2 * IDX_QB], preferred_element_type=F32)
            w = w_ref[0, 0, :, hp * 2 * IDX_QB:(hp + 1) * 2 * IDX_QB] * w_scale
            t = jnp.maximum(t, 0.0) * w
            acc = acc + t[:, :IDX_QB] + t[:, IDX_QB:]
        kchunk = (ks + lax.broadcasted_iota(jnp.int32, (IDX_KC, 1), 0)) // CHUNK
        acc = jnp.where(kchunk <= qchunk, acc, -jnp.inf)
        key_ref[pl.ds(ks, IDX_KC), :] = _sortable_key(acc)
        return carry

    lax.fori_loop(0, n_chunks, score_chunk, 0)

    def count_ge(cand):
        def body(c, acc):
            ks = pl.multiple_of(c * IDX_KC, IDX_KC)
            blk = key_ref[pl.ds(ks, IDX_KC), :]
            hit = jnp.where(blk >= cand, 1, 0).astype(jnp.int32)
            return acc + jnp.sum(hit.reshape(IDX_KC // 8, 8, IDX_QB), axis=0)
        acc = lax.fori_loop(0, n_chunks, body, jnp.zeros((8, IDX_QB), jnp.int32))
        return jnp.sum(acc, axis=0, keepdims=True)

    def bit_step(i, ans):
        bit = lax.shift_left(jnp.int32(1), 31 - i)
        cand = ans | bit
        cnt = count_ge(cand ^ int_min)
        return jnp.where(cnt >= topk, cand, ans)

    ans = lax.fori_loop(0, 32, bit_step, jnp.zeros((1, IDX_QB), jnp.int32))
    thr = ans ^ int_min
    neg_inf_key = _sortable_key(jnp.full((1, IDX_QB), -jnp.inf, F32))

    def emit(c, carry):
        ks = pl.multiple_of(c * IDX_KC, IDX_KC)
        blk = key_ref[pl.ds(ks, IDX_KC), :]
        sel = (blk >= thr) & (blk > neg_inf_key)
        bias_t = jnp.where(sel, 0.0, MASK_NEG).astype(F32)
        o_ref[0, :, pl.ds(ks, IDX_KC)] = bias_t.T.astype(o_ref.dtype)
        return carry

    lax.fori_loop(0, n_chunks, emit, 0)

    def fill(c, carry):
        ks = pl.multiple_of(c * IDX_KC, IDX_KC)
        o_ref[0, :, pl.ds(ks, IDX_KC)] = jnp.full((IDX_QB, IDX_KC), MASK_NEG, o_ref.dtype)
        return carry

    lax.fori_loop(n_chunks, seq // IDX_KC, fill, 0)


def dsa_selection_bias(ki, qit, wt, topk):
    B, S, _ = ki.shape
    nq = S // IDX_QB
    return pl.pallas_call(
        functools.partial(_indexer_kernel, seq=S, topk=topk),
        grid=(B, nq),
        in_specs=[pl.BlockSpec((1, S, LANES), lambda b, i: (b, 0, 0)),
                  pl.BlockSpec((1, 1, IDX_HD, IDX_HEADS * IDX_QB), lambda b, i: (b, i, 0, 0)),
                  pl.BlockSpec((1, 1, 1, IDX_HEADS * IDX_QB), lambda b, i: (b, i, 0, 0))],
        out_specs=pl.BlockSpec((1, IDX_QB, S), lambda b, i: (b, i, 0)),
        out_shape=jax.ShapeDtypeStruct((B, S, S), BF16),
        scratch_shapes=[pltpu.VMEM((S, IDX_QB), jnp.int32)],
        compiler_params=_params(("parallel", "arbitrary")),
        name="dsa_indexer",
    )(ki, qit, wt)


PEER_TB_ROUTE = 128
STAT_ROWS = 8


def _peer_route_kernel(qt_ref, sk_ref, s_ref, st_ref, top_ref):
    half = PEER_DKEY // 2
    tb = PEER_TB_ROUTE
    for hp in range(2 * PEER_HEADS):
        q = qt_ref[hp * half:(hp + 1) * half, :].astype(BF16)
        s_ref[hp] = jnp.dot(sk_ref[hp], q, preferred_element_type=F32)

    def head_tops(h, carry):
        def extract_pair(i, c):
            xa, xb = c
            ma = jnp.max(xa, axis=0, keepdims=True)
            mb = jnp.max(xb, axis=0, keepdims=True)
            top_ref[2 * h, pl.ds(i, 1), :] = ma
            top_ref[2 * h + 1, pl.ds(i, 1), :] = mb
            return jnp.where(xa == ma, -jnp.inf, xa), jnp.where(xb == mb, -jnp.inf, xb)

        lax.fori_loop(0, PEER_TOPK, extract_pair, (s_ref[2 * h], s_ref[2 * h + 1]))
        return carry

    lax.fori_loop(0, PEER_HEADS, head_tops, 0)

    def candidates(h):
        a = top_ref[2 * h]
        b = top_ref[2 * h + 1]
        return jnp.concatenate([a[0:1, :] + b] + [a[i:i + 1, :] + b[0:8, :] for i in range(1, 8)]
                               + [a[8:PEER_TOPK, :] + b[0:1, :]], axis=0)

    def extract(i, x, mx, z):
        m = jnp.max(x, axis=0, keepdims=True)
        mx = jnp.where(i == 0, m, mx)
        return jnp.where(x == m, -jnp.inf, x), mx, z + jnp.exp(m - mx), m

    def pair_stats(hh, carry):
        h0, h1 = 2 * hh, 2 * hh + 1

        def extract2(i, c):
            x0, mx0, z0, _, x1, mx1, z1, _ = c
            return extract(i, x0, mx0, z0) + extract(i, x1, mx1, z1)

        zero = jnp.zeros((1, tb), F32)
        res = lax.fori_loop(0, PEER_TOPK, extract2, (candidates(h0), zero, zero, zero, candidates(h1), zero, zero, zero))
        for h, (_, _, z, last) in ((h0, res[:4]), (h1, res[4:])):
            st_ref[h, 0:1, :] = last
            st_ref[h, 1:2, :] = top_ref[2 * h, 0:1, :]
            st_ref[h, 2:3, :] = top_ref[2 * h + 1, 0:1, :]
            st_ref[h, 3:4, :] = 1.0 / z
            st_ref[h, 4:STAT_ROWS, :] = jnp.zeros((STAT_ROWS - 4, tb), F32)
        return carry

    lax.fori_loop(0, PEER_HEADS // 2, pair_stats, 0)


def peer_route(qt, sub_keys):
    R, T = qt.shape
    tb = PEER_TB_ROUTE
    nsub = 2 * PEER_HEADS
    return pl.pallas_call(
        _peer_route_kernel,
        grid=(T // tb,),
        in_specs=[pl.BlockSpec((R, tb), lambda i: (0, i)),
                  pl.BlockSpec((nsub, N_KEYS, PEER_DKEY // 2), lambda i: (0, 0, 0))],
        out_specs=[pl.BlockSpec((nsub, N_KEYS, tb), lambda i: (0, 0, i)),
                   pl.BlockSpec((PEER_HEADS, STAT_ROWS, tb), lambda i: (0, 0, i))],
        out_shape=[jax.ShapeDtypeStruct((nsub, N_KEYS, T), F32),
                   jax.ShapeDtypeStruct((PEER_HEADS, STAT_ROWS, T), F32)],
        scratch_shapes=[pltpu.VMEM((nsub, PEER_TOPK, tb), F32)],
        compiler_params=_params(("parallel",)),
        name="peer_route",
    )(qt, sub_keys)


PEER_TB = 512
PEER_EB = 512


def _gelu_exact(x):
    return 0.5 * x * (1.0 + lax.erf(x * (2.0 ** -0.5)))


def _peer_dense_kernel(xt_ref, u_ref, v_ref, s_ref, st_ref, y_ref, e2_ref, ht_ref):
    e = pl.program_id(1)

    @pl.when(e == 0)
    def _():
        y_ref[...] = jnp.zeros(y_ref.shape, F32)
        for h in range(PEER_HEADS):
            e2_ref[h] = jnp.exp(s_ref[2 * h + 1] - st_ref[h, 2:3, :])

    act = _gelu_exact(jnp.dot(u_ref[...], xt_ref[...], preferred_element_type=F32))
    rows_per_step = PEER_EB // N_KEYS
    for r in range(rows_per_step):
        i1 = e * rows_per_step + r
        gate = jnp.zeros((N_KEYS, PEER_TB), F32)
        for h in range(PEER_HEADS):
            s1 = s_ref[2 * h, pl.ds(i1, 1), :]
            e1 = jnp.exp(s1 - st_ref[h, 1:2, :]) * st_ref[h, 3:4, :]
            pair = s_ref[2 * h + 1] + s1
            gate = gate + jnp.where(pair >= st_ref[h, 0:1, :], e2_ref[h] * e1, 0.0)
        ht_ref[r * N_KEYS:(r + 1) * N_KEYS, :] = (gate * act[r * N_KEYS:(r + 1) * N_KEYS, :]).astype(BF16)
    y_ref[...] += lax.dot_general(ht_ref[...], v_ref[...], (((0,), (0,)), ((), ())), preferred_element_type=F32)


def peer_dense(xt, u, v, s_t, stats):
    D, T = xt.shape
    E = u.shape[0]
    nsub = 2 * PEER_HEADS
    once = pl.Buffered(1)
    return pl.pallas_call(
        _peer_dense_kernel,
        grid=(T // PEER_TB, E // PEER_EB),
        in_specs=[pl.BlockSpec((D, PEER_TB), lambda i, e: (0, i), pipeline_mode=once),
                  pl.BlockSpec((PEER_EB, D), lambda i, e: (e, 0)),
                  pl.BlockSpec((PEER_EB, D), lambda i, e: (e, 0)),
                  pl.BlockSpec((nsub, N_KEYS, PEER_TB), lambda i, e: (0, 0, i), pipeline_mode=once),
                  pl.BlockSpec((PEER_HEADS, STAT_ROWS, PEER_TB), lambda i, e: (0, 0, i))],
        out_specs=pl.BlockSpec((PEER_TB, D), lambda i, e: (i, 0)),
        out_shape=jax.ShapeDtypeStruct((T, D), F32),
        scratch_shapes=[pltpu.VMEM((PEER_HEADS, N_KEYS, PEER_TB), F32), pltpu.VMEM((PEER_EB, PEER_TB), BF16)],
        compiler_params=_params(("parallel", "arbitrary")),
        name="peer_dense",
    )(xt, u, v, s_t, stats)


def _pad_cols(w, width):
    return jnp.pad(w, ((0, 0), (0, width - w.shape[1])))


def mixer_ab(hb, batch, seq, w_in, rel_bias, q_norm, w_uq, kv_norm, w_ukv, w_out):
    T, D = hb.shape
    a_w = A_HEADS * HEAD_DIM
    qa_scale = HEAD_DIM ** -0.5 * LOG2E
    w_qkv = jnp.concatenate([w_in[:, :a_w] * qa_scale, w_in[:, a_w:3 * a_w]], axis=1).astype(BF16)
    c_cols = MLA_Q_LORA + MLA_KV_LORA + LANES
    w_c = _pad_cols(w_in[:, 3 * a_w:], _round_up(c_cols, 512)).astype(BF16)
    qkv = matmul(hb, w_qkv, BF16)
    cproj = matmul(hb, w_c, F32)

    oa = band_attention(qkv.reshape(batch, seq, 3 * a_w), band_bias_tiles(rel_bias))

    qh = MLA_NOPE + MLA_ROPE
    w_uq_p = jnp.pad(w_uq.reshape(MLA_Q_LORA, B_HEADS, qh), ((0, 0), (0, 0), (0, 2 * LANES - qh)))
    w_uq_p = w_uq_p.reshape(MLA_Q_LORA, B_HEADS * 2 * LANES).astype(BF16)
    cq_n = rms_norm_cols(cproj, 0, MLA_Q_LORA, q_norm)
    ckv_n = rms_norm_cols(cproj, MLA_Q_LORA // MLA_KV_LORA, MLA_KV_LORA, kv_norm)
    q_lat = matmul(cq_n, w_uq_p, F32)
    q_scale = (MLA_NOPE + MLA_ROPE) ** -0.5 * LOG2E
    q_cat = rope_cols(q_lat, 0, B_HEADS, 2 * LANES, MLA_ROPE // 2,
                      rope_tables(seq, MLA_ROPE, 2 * LANES, offset=MLA_NOPE, scale=q_scale), seq)
    kv = matmul(ckv_n, w_ukv.astype(BF16), BF16)
    k_cat, v_cat = mla_assemble_kv(kv, cproj, (MLA_Q_LORA + MLA_KV_LORA) // LANES,
                                   rope_tables(seq, MLA_ROPE, LANES), seq)
    ob = flash_attention(q_cat.reshape(batch, seq, -1), k_cat.reshape(batch, seq, -1), v_cat.reshape(batch, seq, -1),
                         n_heads=B_HEADS, hq=4, hk=4, dk=2 * LANES, dv=MLA_V, q_blk=0, k_blk=0, v_blk=0,
                         mode="chunk")
    return matmul_cat(oa.reshape(T, -1), ob.reshape(T, -1), w_out.astype(BF16), F32)


def mixer_cd(hb, batch, seq, w_in, forget_bias, w_out):
    T, D = hb.shape
    c_qw, c_kw, d_w = C_HEADS * HEAD_DIM, C_KV_HEADS * HEAD_DIM, D_HEADS * HEAD_DIM
    i_w = IDX_HEADS * IDX_HD
    offs = np.cumsum([0, c_qw, c_kw, c_kw, i_w, IDX_HD, IDX_HEADS, d_w, d_w, d_w, D_HEADS])
    col = lambda n: w_in[:, offs[n]:offs[n + 1]]
    q_scale = HEAD_DIM ** -0.5 * LOG2E
    w_a = jnp.concatenate([col(0) * q_scale, col(1), col(2), col(6) * q_scale, col(7), col(8)], axis=1).astype(BF16)
    w_b = jnp.concatenate([col(3), _pad_cols(col(4), LANES), _pad_cols(col(5), LANES), _pad_cols(col(9), LANES)], axis=1)
    w_b = _pad_cols(w_b, _round_up(w_b.shape[1], 512)).astype(BF16)
    proj_a = matmul(hb, w_a, BF16)
    proj_b = matmul(hb, w_b, F32)
    qk_w = c_qw + c_kw
    blk_ki, blk_wi, blk_fd = i_w // LANES, i_w // LANES + 1, i_w // LANES + 2

    qk_rot = rope_cols(proj_a, 0, qk_w // LANES, LANES, HEAD_DIM // 2, rope_tables(seq, HEAD_DIM, LANES), seq)
    qi_rot = rope_cols(proj_b, 0, i_w // LANES, LANES, IDX_HD // 2, rope_tables(seq, IDX_HD, LANES, reps=2), seq)
    ki_rot = rope_cols(proj_b, blk_ki, 1, LANES, IDX_HD // 2, rope_tables(seq, IDX_HD, LANES), seq)
    nq = seq // IDX_QB
    qit = qi_rot.reshape(batch, nq, IDX_QB, IDX_HEADS, IDX_HD).transpose(0, 1, 4, 3, 2)
    qit = qit.reshape(batch, nq, IDX_HD, IDX_HEADS * IDX_QB)
    wi = proj_b[:, blk_wi * LANES:blk_wi * LANES + IDX_HEADS]
    wt = wi.reshape(batch, nq, IDX_QB, IDX_HEADS).transpose(0, 1, 3, 2).reshape(batch, nq, 1, IDX_HEADS * IDX_QB)
    sel_bias = dsa_selection_bias(ki_rot.reshape(batch, seq, LANES), qit, wt, min(DSA_TOPK_MAX, seq // 4))
    qk3 = qk_rot.reshape(batch, seq, qk_w)
    pa3 = proj_a.reshape(batch, seq, -1)
    rep = C_HEADS // C_KV_HEADS
    oc = flash_attention(qk3, qk3, pa3, n_heads=C_HEADS, hq=rep, hk=1, dk=HEAD_DIM, dv=HEAD_DIM,
                         q_blk=0, k_blk=c_qw // HEAD_DIM, v_blk=qk_w // HEAD_DIM, mode="bias", bias=sel_bias)

    fbias = _pad_cols(forget_bias.reshape(1, D_HEADS).astype(F32), LANES)
    cum = fox_cumsum(proj_b, blk_fd, fbias, batch, seq)[:, :, :D_HEADS]
    cum_t = cum.transpose(0, 2, 1)
    hd = 4
    d0 = (qk_w + c_kw) // (hd * HEAD_DIM)
    od = flash_attention(pa3, pa3, pa3, n_heads=D_HEADS, hq=hd, hk=hd, dk=HEAD_DIM, dv=HEAD_DIM,
                         q_blk=d0, k_blk=d0 + d_w // (hd * HEAD_DIM), v_blk=d0 + 2 * d_w // (hd * HEAD_DIM),
                         mode="causal", fox=(cum_t[..., None], cum_t[:, :, None, :]))
    return matmul_cat(oc.reshape(T, -1), od.reshape(T, -1), w_out.astype(BF16), F32)


def memory_cross_attention(hb, mem_b, batch, seq, w_q, w_kv, w_o):
    T, D = hb.shape
    mem_w = MEM_HEADS * MEM_HD
    q_scale = MEM_HD ** -0.5 * LOG2E
    q = matmul(hb, (w_q * q_scale).astype(BF16), BF16)
    kv = matmul(mem_b.reshape(-1, D), w_kv.astype(BF16), BF16)
    kv3 = kv.reshape(batch, -1, 2 * mem_w)
    o = flash_attention(q.reshape(batch, seq, mem_w), kv3, kv3, n_heads=MEM_HEADS, hq=MEM_HEADS, hk=MEM_HEADS,
                        dk=MEM_HD, dv=MEM_HD, q_blk=0, k_blk=0, v_blk=1, mode="none", kb=kv3.shape[1])
    return matmul(o.reshape(T, mem_w), w_o.astype(BF16), F32)


def peer_ffn(hb, w_q, sub_keys, u_tabs, v_tabs, layer):
    ht = hb.T
    qt = matmul(w_q.T.astype(BF16), ht, F32)
    sk = sub_keys.reshape(2 * PEER_HEADS, N_KEYS, PEER_DKEY // 2).astype(BF16)
    s_t, stats = peer_route(qt, sk)
    return peer_dense(ht, cast_layer(u_tabs, layer), cast_layer(v_tabs, layer), s_t, stats)


def kernel(x, mem, ab_w_in, a_rel_bias, b_q_norm, b_w_uq, b_kv_norm, b_w_ukv, ab_w_out, cd_w_in, d_forget_bias,
           cd_w_out, mem_w_q, mem_w_kv, mem_w_o, peer_w_q, peer_sub_keys, peer_u, peer_v, ln_g, ln_b):
    batch, seq, d_model = x.shape
    h = x.reshape(batch * seq, d_model)
    hb = h.astype(BF16)
    mem_b = mem.astype(BF16)
    for layer in range(DEPTH):
        j = layer // 2
        if layer % 2 == 0:
            y = mixer_ab(hb, batch, seq, ab_w_in[j], a_rel_bias[j], b_q_norm[j], b_w_uq[j], b_kv_norm[j],
                         b_w_ukv[j], ab_w_out[j])
        else:
            y = mixer_cd(hb, batch, seq, cd_w_in[j], d_forget_bias[j], cd_w_out[j])
        h, hb = deepnorm_ln(h, y, ln_g[layer, 0], ln_b[layer, 0])
        y = memory_cross_attention(hb, mem_b, batch, seq, mem_w_q[layer], mem_w_kv[layer], mem_w_o[layer])
        h, hb = deepnorm_ln(h, y, ln_g[layer, 1], ln_b[layer, 1])
        y = peer_ffn(hb, peer_w_q[layer], peer_sub_keys[layer], peer_u, peer_v, layer)
        h, hb = deepnorm_ln(h, y, ln_g[layer, 2], ln_b[layer, 2])
    return h.reshape(batch, seq, d_model)
```

```python
import functools

import jax
import jax.numpy as jnp
import numpy as np
from jax import lax
from jax.experimental import pallas as pl
from jax.experimental.pallas import tpu as pltpu

F32 = jnp.float32
BF16 = jnp.bfloat16

CHUNK = 64
HEAD_DIM = 128
A_HEADS = 16
LEFT_CHUNKS = 8
REL_CLIP = 256
B_HEADS = 16
MLA_Q_LORA = 1536
MLA_KV_LORA = 512
MLA_NOPE = 128
MLA_ROPE = 64
MLA_V = 128
C_HEADS = 16
C_KV_HEADS = 4
IDX_HEADS = 32
IDX_HD = 64
DSA_TOPK_MAX = 256
D_HEADS = 16
MEM_HEADS = 4
MEM_HD = 128
PEER_HEADS = 8
PEER_DKEY = 256
N_KEYS = 128
PEER_TOPK = 16
ROPE_THETA = 10000.0
LN_EPS = 1e-5
RMS_EPS = 1e-6
DEPTH = 2
DEEPNORM_ALPHA = (2 * DEPTH) ** 0.25

LANES = 128
V7X_VMEM_LIMIT = 56 * 1024 * 1024
MASK_NEG = -1e30
LOG2E = 1.4426950408889634


def _params(sem, vmem=V7X_VMEM_LIMIT, flags=None):
    return pltpu.CompilerParams(dimension_semantics=sem, vmem_limit_bytes=vmem, flags=flags)


def _round_up(n, m):
    return (n + m - 1) // m * m


def _mm_kernel(a_ref, w_ref, o_ref):
    o_ref[...] = jnp.dot(a_ref[...], w_ref[...], preferred_element_type=F32).astype(o_ref.dtype)


def matmul(a, w, out_dtype, tm=1024, tn=512):
    M, K = a.shape
    N = w.shape[1]
    tm = min(tm, M)
    tn = min(tn, N)
    assert M % tm == 0 and N % tn == 0, (M, N, tm, tn)
    return pl.pallas_call(
        _mm_kernel,
        grid=(M // tm, N // tn),
        in_specs=[pl.BlockSpec((tm, K), lambda i, j: (i, 0)),
                  pl.BlockSpec((K, tn), lambda i, j: (0, j))],
        out_specs=pl.BlockSpec((tm, tn), lambda i, j: (i, j)),
        out_shape=jax.ShapeDtypeStruct((M, N), out_dtype),
        compiler_params=_params(("parallel", "arbitrary")),
        name="matmul",
    )(a, w)


def _mm2_kernel(a1_ref, a2_ref, w_ref, o_ref):
    a = jnp.concatenate([a1_ref[...], a2_ref[...]], axis=1)
    o_ref[...] = jnp.dot(a, w_ref[...], preferred_element_type=F32).astype(o_ref.dtype)


def matmul_cat(a1, a2, w, out_dtype, tm=1024, tn=512):
    M, K1 = a1.shape
    K2 = a2.shape[1]
    N = w.shape[1]
    tm = min(tm, M)
    tn = min(tn, N)
    assert M % tm == 0 and N % tn == 0 and w.shape[0] == K1 + K2
    return pl.pallas_call(
        _mm2_kernel,
        grid=(M // tm, N // tn),
        in_specs=[pl.BlockSpec((tm, K1), lambda i, j: (i, 0)),
                  pl.BlockSpec((tm, K2), lambda i, j: (i, 0)),
                  pl.BlockSpec((K1 + K2, tn), lambda i, j: (0, j))],
        out_specs=pl.BlockSpec((tm, tn), lambda i, j: (i, j)),
        out_shape=jax.ShapeDtypeStruct((M, N), out_dtype),
        compiler_params=_params(("parallel", "arbitrary")),
        name="matmul_cat",
    )(a1, a2, w)


def _cast_kernel(u_ref, v_ref, ou_ref, ov_ref):
    ou_ref[...] = u_ref[...].astype(ou_ref.dtype)
    ov_ref[...] = v_ref[...].astype(ov_ref.dtype)


def cast_layer_pair(u, v, layer, tr=256):
    _, R, C = u.shape
    src = pl.BlockSpec((None, tr, C), lambda i: (layer, i, 0))
    dst = pl.BlockSpec((tr, C), lambda i: (i, 0))
    return pl.pallas_call(
        _cast_kernel,
        grid=(R // tr,),
        in_specs=[src, src],
        out_specs=[dst, dst],
        out_shape=[jax.ShapeDtypeStruct((R, C), BF16)] * 2,
        compiler_params=_params(("parallel",)),
        name="cast_layer_pair",
    )(u, v)


def _ln_kernel(h_ref, y_ref, g_ref, b_ref, o_ref, ob_ref):
    z = DEEPNORM_ALPHA * h_ref[...] + y_ref[...]
    zc = z - jnp.mean(z, axis=-1, keepdims=True)
    var = jnp.mean(zc * zc, axis=-1, keepdims=True)
    out = zc * lax.rsqrt(var + LN_EPS) * g_ref[...] + b_ref[...]
    o_ref[...] = out
    ob_ref[...] = out.astype(BF16)


def deepnorm_ln(h, y, g, b, tb=256):
    T, D = h.shape
    row = pl.BlockSpec((tb, D), lambda i: (i, 0))
    vec = pl.BlockSpec((1, D), lambda i: (0, 0))
    return pl.pallas_call(
        _ln_kernel,
        grid=(T // tb,),
        in_specs=[row, row, vec, vec],
        out_specs=[row, row],
        out_shape=[jax.ShapeDtypeStruct((T, D), F32), jax.ShapeDtypeStruct((T, D), BF16)],
        compiler_params=_params(("parallel",)),
        name="deepnorm_ln",
    )(h, y, g.reshape(1, D), b.reshape(1, D))


def _mm_ln_kernel(a_ref, w_ref, h_ref, g_ref, b_ref, o_ref, obt_ref):
    z = DEEPNORM_ALPHA * h_ref[...] + jnp.dot(a_ref[...], w_ref[...], preferred_element_type=F32)
    zc = z - jnp.mean(z, axis=-1, keepdims=True)
    var = jnp.mean(zc * zc, axis=-1, keepdims=True)
    out = zc * lax.rsqrt(var + LN_EPS) * g_ref[...] + b_ref[...]
    o_ref[...] = out
    obt_ref[...] = out.T.astype(BF16)


def matmul_deepnorm_ln(a, w, h, g, b, tb=256):
    T, D = h.shape
    K = a.shape[1]
    row = pl.BlockSpec((tb, D), lambda i: (i, 0))
    vec = pl.BlockSpec((1, D), lambda i: (0, 0))
    return pl.pallas_call(
        _mm_ln_kernel,
        grid=(T // tb,),
        in_specs=[pl.BlockSpec((tb, K), lambda i: (i, 0)), pl.BlockSpec((K, D), lambda i: (0, 0)), row, vec, vec],
        out_specs=[row, pl.BlockSpec((D, tb), lambda i: (0, i))],
        out_shape=[jax.ShapeDtypeStruct((T, D), F32), jax.ShapeDtypeStruct((D, T), BF16)],
        compiler_params=_params(("parallel",)),
        name="matmul_deepnorm_ln",
    )(a, w, h, g.reshape(1, D), b.reshape(1, D))


def _rms_kernel(x_ref, g_ref, o_ref):
    x = x_ref[...].astype(F32)
    o_ref[...] = (x * lax.rsqrt(jnp.mean(x * x, axis=-1, keepdims=True) + RMS_EPS) * g_ref[...]).astype(o_ref.dtype)


def rms_norm_cols(x, col_block, width, g, tb=512):
    T = x.shape[0]
    return pl.pallas_call(
        _rms_kernel,
        grid=(T // tb,),
        in_specs=[pl.BlockSpec((tb, width), lambda i: (i, col_block)),
                  pl.BlockSpec((1, width), lambda i: (0, 0))],
        out_specs=pl.BlockSpec((tb, width), lambda i: (i, 0)),
        out_shape=jax.ShapeDtypeStruct((T, width), BF16),
        compiler_params=_params(("parallel",)),
        name="rms_norm",
    )(x, g.reshape(1, width))


def rope_tables(seq, dim, group, offset=0, reps=1, scale=1.0):
    half = dim // 2
    inv = ROPE_THETA ** (-jnp.arange(half, dtype=F32) * 2.0 / dim)
    ang = jnp.arange(seq, dtype=F32)[:, None] * inv[None, :]
    cos, sin = jnp.cos(ang), jnp.sin(ang)
    zeros_tail = jnp.zeros((seq, group - offset - dim * reps), F32)
    head = jnp.ones((seq, offset), F32)
    zhead = jnp.zeros((seq, offset), F32)
    zhalf = jnp.zeros((seq, half), F32)
    c = jnp.concatenate([head] + [cos, cos] * reps + [zeros_tail], axis=1)
    sa = jnp.concatenate([zhead] + [-sin, zhalf] * reps + [zeros_tail], axis=1)
    sb = jnp.concatenate([zhead] + [zhalf, sin] * reps + [zeros_tail], axis=1)
    return c * scale, sa * scale, sb * scale


def _rope_kernel(x_ref, c_ref, sa_ref, sb_ref, o_ref, *, groups, group, half):
    c, sa, sb = c_ref[...], sa_ref[...], sb_ref[...]
    for g in range(groups):
        x = x_ref[:, g * group:(g + 1) * group].astype(F32)
        out = x * c + pltpu.roll(x, group - half, 1) * sa + pltpu.roll(x, half, 1) * sb
        o_ref[:, g * group:(g + 1) * group] = out.astype(o_ref.dtype)


def rope_cols(x, col_block, groups, group, half, tables, seq, tb=256):
    T = x.shape[0]
    W = groups * group
    nsb = seq // tb
    tab = pl.BlockSpec((tb, group), lambda i: (i % nsb, 0))
    return pl.pallas_call(
        functools.partial(_rope_kernel, groups=groups, group=group, half=half),
        grid=(T // tb,),
        in_specs=[pl.BlockSpec((tb, W), lambda i: (i, col_block)), tab, tab, tab],
        out_specs=pl.BlockSpec((tb, W), lambda i: (i, 0)),
        out_shape=jax.ShapeDtypeStruct((T, W), BF16),
        compiler_params=_params(("parallel",)),
        name="rope",
    )(x, *tables)


def _mla_kv_kernel(kv_ref, kr_ref, c_ref, sa_ref, sb_ref, k_ref, v_ref, *, heads, half):
    kr = kr_ref[...].astype(F32)
    kr = kr * c_ref[...] + pltpu.roll(kr, LANES - half, 1) * sa_ref[...] + pltpu.roll(kr, half, 1) * sb_ref[...]
    kr = kr.astype(k_ref.dtype)
    for h in range(heads):
        k_ref[:, 2 * h * LANES:(2 * h + 1) * LANES] = kv_ref[:, 2 * h * LANES:(2 * h + 1) * LANES]
        k_ref[:, (2 * h + 1) * LANES:(2 * h + 2) * LANES] = kr
        v_ref[:, h * LANES:(h + 1) * LANES] = kv_ref[:, (2 * h + 1) * LANES:(2 * h + 2) * LANES]


def mla_assemble_kv(kv, cproj, kr_col_block, tables, seq, tb=256):
    T, W = kv.shape
    nsb = seq // tb
    tab = pl.BlockSpec((tb, LANES), lambda i: (i % nsb, 0))
    return pl.pallas_call(
        functools.partial(_mla_kv_kernel, heads=B_HEADS, half=MLA_ROPE // 2),
        grid=(T // tb,),
        in_specs=[pl.BlockSpec((tb, W), lambda i: (i, 0)),
                  pl.BlockSpec((tb, LANES), lambda i: (i, kr_col_block)), tab, tab, tab],
        out_specs=[pl.BlockSpec((tb, W), lambda i: (i, 0)), pl.BlockSpec((tb, W // 2), lambda i: (i, 0))],
        out_shape=[jax.ShapeDtypeStruct((T, W), BF16), jax.ShapeDtypeStruct((T, W // 2), BF16)],
        compiler_params=_params(("parallel",)),
        name="mla_assemble_kv",
    )(kv, cproj, *tables)


FLASH_RB = 128


def _flash_kernel(*refs, hq, hk, dk, dv, qb, kb, mode, fox, q_axis, n_kv_blocks):
    it = iter(refs)
    q_ref, k_ref, v_ref = next(it), next(it), next(it)
    ct_ref = cs_ref = bias_ref = None
    if fox:
        ct_ref, cs_ref = next(it), next(it)
    if mode == "bias":
        bias_ref = next(it)
    o_ref, m_ref, l_ref, acc_ref, p_ref, alpha_ref = next(it), next(it), next(it), next(it), next(it), next(it)
    ct_lanes_ref = next(it) if fox else None

    qi = pl.program_id(q_axis)
    m_ref[...] = jnp.full(m_ref.shape, MASK_NEG, F32)
    l_ref[...] = jnp.zeros(l_ref.shape, F32)
    acc_ref[...] = jnp.zeros(acc_ref.shape, F32)
    if fox:
        for h in range(hq):
            ct_lanes_ref[h] = jnp.broadcast_to(ct_ref[0, h] * LOG2E, (qb, LANES))

    n_sub = qb // FLASH_RB
    per_kv = kb // FLASH_RB
    rows = lax.broadcasted_iota(jnp.int32, (FLASH_RB, kb), 0)
    cols = lax.broadcasted_iota(jnp.int32, (FLASH_RB, kb), 1)

    def diag_mask(r):
        row_in_kv = rows + (r % per_kv) * FLASH_RB
        if mode == "causal":
            return cols <= row_in_kv
        if mode == "chunk":
            return (cols // CHUNK) <= (row_in_kv // CHUNK)
        return None

    def step(j, visible):
        ks = pl.multiple_of(j * kb, kb)
        for h in range(hq):
            g = h if hk == hq else 0
            k = k_ref[0, pl.ds(ks, kb), g * dk:(g + 1) * dk]
            v = v_ref[0, pl.ds(ks, kb), g * dv:(g + 1) * dv]
            s_all = lax.dot_general(q_ref[0, :, h * dk:(h + 1) * dk], k, (((1,), (1,)), ((), ())),
                                    preferred_element_type=F32)
            cs = cs_ref[0, h, :, pl.ds(ks, kb)] * LOG2E if fox else None
            for r in range(n_sub):
                rs = slice(r * FLASH_RB, (r + 1) * FLASH_RB)
                if visible[r] == "none":
                    p_ref[h, rs, :] = jnp.zeros((FLASH_RB, kb), p_ref.dtype)
                    alpha_ref[h, rs, :] = jnp.ones((FLASH_RB, LANES), F32)
                    continue
                s = s_all[rs]
                if bias_ref is not None:
                    s = s + bias_ref[0, rs, pl.ds(ks, kb)].astype(F32)
                blocks = [s[:, c * LANES:(c + 1) * LANES] for c in range(kb // LANES)]
                if fox:
                    ct = ct_lanes_ref[h, rs]
                    blocks = [blk + ct - cs[:, c * LANES:(c + 1) * LANES] for c, blk in enumerate(blocks)]
                if visible[r] == "diag" and mode in ("causal", "chunk"):
                    mask = diag_mask(r)
                    blocks = [jnp.where(mask[:, c * LANES:(c + 1) * LANES], blk, MASK_NEG)
                              for c, blk in enumerate(blocks)]
                blk_max = blocks[0]
                for blk in blocks[1:]:
                    blk_max = jnp.maximum(blk_max, blk)
                m_prev = m_ref[h, rs]
                m_new = jnp.maximum(m_prev, jnp.max(blk_max, axis=1, keepdims=True))
                alpha = jnp.exp2(m_prev - m_new)
                p_blocks = [jnp.exp2(blk - m_new) for blk in blocks]
                p_lanes = p_blocks[0]
                for pb in p_blocks[1:]:
                    p_lanes = p_lanes + pb
                l_ref[h, rs] = alpha * l_ref[h, rs] + p_lanes
                for c, pb in enumerate(p_blocks):
                    p_ref[h, rs, c * LANES:(c + 1) * LANES] = pb.astype(p_ref.dtype)
                alpha_ref[h, rs, :] = alpha
                m_ref[h, rs] = m_new
            pv = jnp.dot(p_ref[h], v, preferred_element_type=F32)
            acc_ref[h] = alpha_ref[h] * acc_ref[h] + pv

    def body(j, carry):
        step(j, ["full"] * n_sub)
        return carry

    if mode == "none":
        lax.fori_loop(0, n_kv_blocks, body, 0)
    else:
        kv_per_q = qb // kb
        lax.fori_loop(0, qi * kv_per_q, body, 0)
        for t in range(kv_per_q):
            kinds = ["full" if r // per_kv > t else "diag" if r // per_kv == t else "none" for r in range(n_sub)]
            step(qi * kv_per_q + t, kinds)

    for h in range(hq):
        l = jnp.sum(l_ref[h], axis=1, keepdims=True)
        o_ref[0, :, h * dv:(h + 1) * dv] = (acc_ref[h] / l).astype(o_ref.dtype)


def flash_attention(q, k, v, *, n_heads, hq, hk, dk, dv, q_blk, k_blk, v_blk, mode,
                    qb=512, kb=256, fox=None, bias=None):
    B, Sq = q.shape[0], q.shape[1]
    Skv = k.shape[1]
    kb = min(kb, Skv)
    nq, ng = Sq // qb, n_heads // hq
    assert hk in (hq, 1) and qb % kb == 0 and kb % FLASH_RB == 0
    if mode in ("causal", "chunk", "bias"):
        assert Sq == Skv
    if mode != "bias":
        grid = (B, ng, nq)
        q_axis = 2

        def spec(shape, fn):
            return pl.BlockSpec(shape, lambda b, g, i: fn(b, g, i))
    else:
        grid = (B, nq, ng)
        q_axis = 1

        def spec(shape, fn):
            return pl.BlockSpec(shape, lambda b, i, g: fn(b, g, i))

    in_specs = [
        spec((1, qb, hq * dk), lambda b, g, i: (b, i, q_blk + g)),
        spec((1, Skv, hk * dk), lambda b, g, i: (b, 0, k_blk + g)),
        spec((1, Skv, hk * dv), lambda b, g, i: (b, 0, v_blk + g)),
    ]
    args = [q, k, v]
    if fox is not None:
        ct, cs = fox
        in_specs += [spec((1, hq, qb, 1), lambda b, g, i: (b, g, i, 0)),
                     spec((1, hq, 1, Skv), lambda b, g, i: (b, g, 0, 0))]
        args += [ct, cs]
    if mode == "bias":
        in_specs.append(spec((1, qb, Skv), lambda b, g, i: (b, i, 0)))
        args.append(bias)
    assert dv == LANES
    stat = pltpu.VMEM((hq, qb, LANES), F32)
    kern = functools.partial(_flash_kernel, hq=hq, hk=hk, dk=dk, dv=dv, qb=qb, kb=kb, mode=mode,
                             fox=fox is not None, q_axis=q_axis, n_kv_blocks=Skv // kb)
    return pl.pallas_call(
        kern,
        grid=grid,
        in_specs=in_specs,
        out_specs=spec((1, qb, hq * dv), lambda b, g, i: (b, i, g)),
        out_shape=jax.ShapeDtypeStruct((B, Sq, n_heads * dv), BF16),
        scratch_shapes=[stat, stat, pltpu.VMEM((hq, qb, dv), F32), pltpu.VMEM((hq, qb, kb), BF16), stat]
        + ([stat] if fox is not None else []),
        compiler_params=_params(("parallel", "parallel", "arbitrary")),
        name="flash_" + mode,
    )(*args)


BAND_QB = 4 * CHUNK
BAND_KBLOCKS = (LEFT_CHUNKS * CHUNK) // BAND_QB + 1
BAND_W = BAND_KBLOCKS * BAND_QB


def band_bias_tiles(rel_bias):
    pad = LEFT_CHUNKS * CHUNK
    period = BAND_QB + BAND_W
    m = np.arange(period)
    j_minus_i = np.where(m < period - BAND_QB, m, m - period)
    r = rel_bias[:, np.clip(pad - j_minus_i, -REL_CLIP, REL_CLIP) + REL_CLIP].astype(F32)
    skew = jnp.tile(r, (1, BAND_QB))[:, :BAND_QB * (period - 1)].reshape(-1, BAND_QB, period - 1)
    i = np.arange(BAND_QB)[:, None]
    j = np.arange(BAND_W)[None, :]
    chunk_diff = i // CHUNK + LEFT_CHUNKS - j // CHUNK
    in_band = (chunk_diff >= 0) & (chunk_diff <= LEFT_CHUNKS)
    return jnp.where(in_band[None], skew[:, :, :BAND_W] * LOG2E, MASK_NEG)


def _band_kernel(q_ref, k_ref, v_ref, bias_ref, o_ref, s_ref, p_ref, l_ref, *, hb):
    qi = pl.program_id(2)
    for h in range(hb):
        lanes = slice(h * HEAD_DIM, (h + 1) * HEAD_DIM)
        q = q_ref[0, :, lanes]
        v_parts = []
        for jb in range(BAND_KBLOCKS):
            kblk = qi - (BAND_KBLOCKS - 1) + jb
            ks = pl.multiple_of(jnp.maximum(kblk, 0) * BAND_QB, BAND_QB)
            k = k_ref[0, pl.ds(ks, BAND_QB), lanes]
            v_parts.append(v_ref[0, pl.ds(ks, BAND_QB), lanes])
            s = lax.dot_general(q, k, (((1,), (1,)), ((), ())), preferred_element_type=F32)
            s = s + bias_ref[h, :, jb * BAND_QB:(jb + 1) * BAND_QB]
            s_ref[h, :, jb * BAND_QB:(jb + 1) * BAND_QB] = jnp.where(kblk >= 0, s, MASK_NEG)
        for r in range(BAND_QB // CHUNK):
            rs = slice(r * CHUNK, (r + 1) * CHUNK)
            blocks = [s_ref[h, rs, c * LANES:(c + 1) * LANES] for c in range(BAND_W // LANES)]
            blk_max = blocks[0]
            for blk in blocks[1:]:
                blk_max = jnp.maximum(blk_max, blk)
            m = jnp.broadcast_to(jnp.max(blk_max, axis=1, keepdims=True), (CHUNK, LANES))
            p_lanes = jnp.zeros((CHUNK, LANES), F32)
            for c, blk in enumerate(blocks):
                p = jnp.exp2(blk - m)
                p_lanes = p_lanes + p
                p_ref[h, rs, c * LANES:(c + 1) * LANES] = p.astype(p_ref.dtype)
            l_ref[h, rs, :] = p_lanes
        acc = jnp.zeros((BAND_QB, HEAD_DIM), F32)
        for jb in range(BAND_KBLOCKS):
            acc = acc + jnp.dot(p_ref[h, :, jb * BAND_QB:(jb + 1) * BAND_QB], v_parts[jb], preferred_element_type=F32)
        l = jnp.sum(l_ref[h], axis=1, keepdims=True)
        o_ref[0, :, lanes] = (acc / l).astype(o_ref.dtype)


def band_attention(qkv, bias_tiles, hb=4):
    B, S, _ = qkv.shape
    H = A_HEADS
    ng = H // hb
    return pl.pallas_call(
        functools.partial(_band_kernel, hb=hb),
        grid=(B, ng, S // BAND_QB),
        in_specs=[pl.BlockSpec((1, BAND_QB, hb * HEAD_DIM), lambda b, g, i: (b, i, g)),
                  pl.BlockSpec((1, S, hb * HEAD_DIM), lambda b, g, i: (b, 0, ng + g)),
                  pl.BlockSpec((1, S, hb * HEAD_DIM), lambda b, g, i: (b, 0, 2 * ng + g)),
                  pl.BlockSpec((hb, BAND_QB, BAND_W), lambda b, g, i: (g, 0, 0))],
        out_specs=pl.BlockSpec((1, BAND_QB, hb * HEAD_DIM), lambda b, g, i: (b, i, g)),
        out_shape=jax.ShapeDtypeStruct((B, S, H * HEAD_DIM), BF16),
        scratch_shapes=[pltpu.VMEM((hb, BAND_QB, BAND_W), F32), pltpu.VMEM((hb, BAND_QB, BAND_W), BF16),
                        pltpu.VMEM((hb, BAND_QB, LANES), F32)],
        compiler_params=_params(("parallel", "parallel", "arbitrary")),
        name="band_attention",
    )(qkv, qkv, qkv, bias_tiles)


def _fox_cumsum_kernel(f_ref, b_ref, o_ref, carry_ref, *, cb):
    @pl.when(pl.program_id(1) == 0)
    def _():
        carry_ref[...] = jnp.zeros(carry_ref.shape, F32)

    x = f_ref[0].astype(F32) + b_ref[...]
    log_f = jnp.minimum(x, 0.0) - jnp.log1p(jnp.exp(-jnp.abs(x)))
    r = lax.broadcasted_iota(jnp.int32, (cb, cb), 0)
    c = lax.broadcasted_iota(jnp.int32, (cb, cb), 1)
    tri = jnp.where(c <= r, 1.0, 0.0).astype(F32)
    cum = jnp.dot(tri, log_f, preferred_element_type=F32, precision=lax.Precision.HIGHEST) + carry_ref[...]
    o_ref[0] = cum
    carry_ref[...] = cum[cb - 1:cb, :]


def fox_cumsum(proj, col_block, bias_row, batch, seq, cb=256):
    x = proj.reshape(batch, seq, proj.shape[-1])
    return pl.pallas_call(
        functools.partial(_fox_cumsum_kernel, cb=cb),
        grid=(batch, seq // cb),
        in_specs=[pl.BlockSpec((1, cb, LANES), lambda b, i: (b, i, col_block)),
                  pl.BlockSpec((1, LANES), lambda b, i: (0, 0))],
        out_specs=pl.BlockSpec((1, cb, LANES), lambda b, i: (b, i, 0)),
        out_shape=jax.ShapeDtypeStruct((batch, seq, LANES), F32),
        scratch_shapes=[pltpu.VMEM((1, LANES), F32)],
        compiler_params=_params(("parallel", "arbitrary")),
        name="fox_cumsum",
    )(x, bias_row)


IDX_QB = 128
IDX_KC = 512


def _sortable_key(x):
    bits = pltpu.bitcast(x, jnp.int32)
    return bits ^ ((bits >> 31) & jnp.int32(0x7FFFFFFF))


def _indexer_kernel(ki_ref, qit_ref, w_ref, o_ref, key_ref, *, seq, topk):
    qi = pl.program_id(1)
    q0 = qi * IDX_QB
    n_chunks = (q0 + IDX_QB + IDX_KC - 1) // IDX_KC
    qchunk = (q0 + lax.broadcasted_iota(jnp.int32, (1, IDX_QB), 1)) // CHUNK
    int_min = jnp.int32(-2 ** 31)
    w_scale = IDX_HEADS ** -0.5 * IDX_HD ** -0.5

    def score_chunk(c, carry):
        ks = pl.multiple_of(c * IDX_KC, IDX_KC)
        ki = ki_ref[0, pl.ds(ks, IDX_KC), :][:, :IDX_HD]
        acc = jnp.zeros((IDX_KC, IDX_QB), F32)
        for hp in range(IDX_HEADS // 2):
            t = jnp.dot(ki, qit_ref[0, 0, :, hp * 2 * IDX_QB:(hp + 1) * 2 * IDX_QB], preferred_element_type=F32)
            w = w_ref[0, 0, :, hp * 2 * IDX_QB:(hp + 1) * 2 * IDX_QB] * w_scale
            t = jnp.maximum(t, 0.0) * w
            acc = acc + t[:, :IDX_QB] + t[:, IDX_QB:]
        kchunk = (ks + lax.broadcasted_iota(jnp.int32, (IDX_KC, 1), 0)) // CHUNK
        acc = jnp.where(kchunk <= qchunk, acc, -jnp.inf)
        key_ref[pl.ds(ks, IDX_KC), :] = _sortable_key(acc)
        return carry

    lax.fori_loop(0, n_chunks, score_chunk, 0)

    def count_ge(cand):
        def body(c, acc):
            ks = pl.multiple_of(c * IDX_KC, IDX_KC)
            blk = key_ref[pl.ds(ks, IDX_KC), :]
            hit = jnp.where(blk >= cand, 1, 0).astype(jnp.int32)
            return acc + jnp.sum(hit.reshape(IDX_KC // 8, 8, IDX_QB), axis=0)
        acc = lax.fori_loop(0, n_chunks, body, jnp.zeros((8, IDX_QB), jnp.int32))
        return jnp.sum(acc, axis=0, keepdims=True)

    def bit_step(i, ans):
        bit = lax.shift_left(jnp.int32(1), 31 - i)
        cand = ans | bit
        cnt = count_ge(cand ^ int_min)
        return jnp.where(cnt >= topk, cand, ans)

    ans = lax.fori_loop(0, 32, bit_step, jnp.zeros((1, IDX_QB), jnp.int32))
    thr = ans ^ int_min
    neg_inf_key = _sortable_key(jnp.full((1, IDX_QB), -jnp.inf, F32))

    def emit(c, carry):
        ks = pl.multiple_of(c * IDX_KC, IDX_KC)
        blk = key_ref[pl.ds(ks, IDX_KC), :]
        sel = (blk >= thr) & (blk > neg_inf_key)
        bias_t = jnp.where(sel, 0.0, MASK_NEG).astype(F32)
        o_ref[0, :, pl.ds(ks, IDX_KC)] = bias_t.T.astype(o_ref.dtype)
        return carry

    lax.fori_loop(0, n_chunks, emit, 0)

    def fill(c, carry):
        ks = pl.multiple_of(c * IDX_KC, IDX_KC)
        o_ref[0, :, pl.ds(ks, IDX_KC)] = jnp.full((IDX_QB, IDX_KC), MASK_NEG, o_ref.dtype)
        return carry

    lax.fori_loop(n_chunks, seq // IDX_KC, fill, 0)


def dsa_selection_bias(ki, qit, wt, topk):
    B, S, _ = ki.shape
    nq = S // IDX_QB
    return pl.pallas_call(
        functools.partial(_indexer_kernel, seq=S, topk=topk),
        grid=(B, nq),
        in_specs=[pl.BlockSpec((1, S, LANES), lambda b, i: (b, 0, 0)),
                  pl.BlockSpec((1, 1, IDX_HD, IDX_HEADS * IDX_QB), lambda b, i: (b, i, 0, 0)),
                  pl.BlockSpec((1, 1, 1, IDX_HEADS * IDX_QB), lambda b, i: (b, i, 0, 0))],
        out_specs=pl.BlockSpec((1, IDX_QB, S), lambda b, i: (b, i, 0)),
        out_shape=jax.ShapeDtypeStruct((B, S, S), BF16),
        scratch_shapes=[pltpu.VMEM((S, IDX_QB), jnp.int32)],
        compiler_params=_params(("parallel", "arbitrary")),
        name="dsa_indexer",
    )(ki, qit, wt)


PEER_TB_ROUTE = 128
STAT_ROWS = 8


def _peer_route_kernel(qt_ref, sk_ref, s_ref, st_ref, top_ref):
    half = PEER_DKEY // 2
    tb = PEER_TB_ROUTE
    for hp in range(2 * PEER_HEADS):
        q = qt_ref[hp * half:(hp + 1) * half, :].astype(BF16)
        s_ref[hp] = jnp.dot(sk_ref[hp], q, preferred_element_type=F32)

    def head_tops(h, carry):
        def extract_pair(i, c):
            xa, xb = c
            ma = jnp.max(xa, axis=0, keepdims=True)
            mb = jnp.max(xb, axis=0, keepdims=True)
            top_ref[2 * h, pl.ds(i, 1), :] = ma
            top_ref[2 * h + 1, pl.ds(i, 1), :] = mb
            return jnp.where(xa == ma, -jnp.inf, xa), jnp.where(xb == mb, -jnp.inf, xb)

        lax.fori_loop(0, PEER_TOPK, extract_pair, (s_ref[2 * h], s_ref[2 * h + 1]))
        return carry

    lax.fori_loop(0, PEER_HEADS, head_tops, 0)

    def candidates(h):
        a = top_ref[2 * h]
        b = top_ref[2 * h + 1]
        return jnp.concatenate([a[0:1, :] + b] + [a[i:i + 1, :] + b[0:8, :] for i in range(1, 8)]
                               + [a[8:PEER_TOPK, :] + b[0:1, :]], axis=0)

    def extract(i, x, mx, z):
        m = jnp.max(x, axis=0, keepdims=True)
        mx = jnp.where(i == 0, m, mx)
        return jnp.where(x == m, -jnp.inf, x), mx, z + jnp.exp(m - mx), m

    def pair_stats(hh, carry):
        h0, h1 = 2 * hh, 2 * hh + 1

        def extract2(i, c):
            x0, mx0, z0, _, x1, mx1, z1, _ = c
            return extract(i, x0, mx0, z0) + extract(i, x1, mx1, z1)

        zero = jnp.zeros((1, tb), F32)
        res = lax.fori_loop(0, PEER_TOPK, extract2, (candidates(h0), zero, zero, zero, candidates(h1), zero, zero, zero))
        for h, (_, _, z, last) in ((h0, res[:4]), (h1, res[4:])):
            st_ref[h, 0:1, :] = last
            st_ref[h, 1:2, :] = top_ref[2 * h, 0:1, :]
            st_ref[h, 2:3, :] = top_ref[2 * h + 1, 0:1, :]
            st_ref[h, 3:4, :] = 1.0 / z
            st_ref[h, 4:STAT_ROWS, :] = jnp.zeros((STAT_ROWS - 4, tb), F32)
        return carry

    lax.fori_loop(0, PEER_HEADS // 2, pair_stats, 0)


def peer_route(qt, sub_keys):
    R, T = qt.shape
    tb = PEER_TB_ROUTE
    nsub = 2 * PEER_HEADS
    return pl.pallas_call(
        _peer_route_kernel,
        grid=(T // tb,),
        in_specs=[pl.BlockSpec((R, tb), lambda i: (0, i)),
                  pl.BlockSpec((nsub, N_KEYS, PEER_DKEY // 2), lambda i: (0, 0, 0))],
        out_specs=[pl.BlockSpec((nsub, N_KEYS, tb), lambda i: (0, 0, i)),
                   pl.BlockSpec((PEER_HEADS, STAT_ROWS, tb), lambda i: (0, 0, i))],
        out_shape=[jax.ShapeDtypeStruct((nsub, N_KEYS, T), F32),
                   jax.ShapeDtypeStruct((PEER_HEADS, STAT_ROWS, T), F32)],
        scratch_shapes=[pltpu.VMEM((nsub, PEER_TOPK, tb), F32)],
        compiler_params=_params(("parallel",)),
        name="peer_route",
    )(qt, sub_keys)


PEER_TB = 512
PEER_EB = 512


def _gelu_exact(x):
    return 0.5 * x * (1.0 + lax.erf(x * (2.0 ** -0.5)))


def _peer_dense_kernel(xt_ref, u_ref, v_ref, s_ref, st_ref, y_ref, e2_ref, ht_ref):
    e = pl.program_id(1)

    @pl.when(e == 0)
    def _():
        y_ref[...] = jnp.zeros(y_ref.shape, F32)
        for h in range(PEER_HEADS):
            e2_ref[h] = jnp.exp(s_ref[2 * h + 1] - st_ref[h, 2:3, :])

    act = _gelu_exact(jnp.dot(u_ref[...], xt_ref[...], preferred_element_type=F32))
    rows_per_step = PEER_EB // N_KEYS
    for r in range(rows_per_step):
        i1 = e * rows_per_step + r
        gate = jnp.zeros((N_KEYS, PEER_TB), F32)
        for h in range(PEER_HEADS):
            s1 = s_ref[2 * h, pl.ds(i1, 1), :]
            e1 = jnp.exp(s1 - st_ref[h, 1:2, :]) * st_ref[h, 3:4, :]
            pair = s_ref[2 * h + 1] + s1
            gate = gate + jnp.where(pair >= st_ref[h, 0:1, :], e2_ref[h] * e1, 0.0)
        ht_ref[r * N_KEYS:(r + 1) * N_KEYS, :] = (gate * act[r * N_KEYS:(r + 1) * N_KEYS, :]).astype(BF16)
    y_ref[...] += lax.dot_general(ht_ref[...], v_ref[...], (((0,), (0,)), ((), ())), preferred_element_type=F32)


def peer_dense(xt, u, v, s_t, stats):
    D, T = xt.shape
    E = u.shape[0]
    nsub = 2 * PEER_HEADS
    once = pl.Buffered(1)
    return pl.pallas_call(
        _peer_dense_kernel,
        grid=(T // PEER_TB, E // PEER_EB),
        in_specs=[pl.BlockSpec((D, PEER_TB), lambda i, e: (0, i), pipeline_mode=once),
                  pl.BlockSpec((PEER_EB, D), lambda i, e: (e, 0)),
                  pl.BlockSpec((PEER_EB, D), lambda i, e: (e, 0)),
                  pl.BlockSpec((nsub, N_KEYS, PEER_TB), lambda i, e: (0, 0, i), pipeline_mode=once),
                  pl.BlockSpec((PEER_HEADS, STAT_ROWS, PEER_TB), lambda i, e: (0, 0, i))],
        out_specs=pl.BlockSpec((PEER_TB, D), lambda i, e: (i, 0)),
        out_shape=jax.ShapeDtypeStruct((T, D), F32),
        scratch_shapes=[pltpu.VMEM((PEER_HEADS, N_KEYS, PEER_TB), F32), pltpu.VMEM((PEER_EB, PEER_TB), BF16)],
        compiler_params=_params(("parallel", "arbitrary")),
        name="peer_dense",
    )(xt, u, v, s_t, stats)


def _pad_cols(w, width):
    return jnp.pad(w, ((0, 0), (0, width - w.shape[1])))


def mixer_ab(hb, batch, seq, w_in, rel_bias, q_norm, w_uq, kv_norm, w_ukv, w_out):
    T, D = hb.shape
    a_w = A_HEADS * HEAD_DIM
    qa_scale = HEAD_DIM ** -0.5 * LOG2E
    w_qkv = jnp.concatenate([w_in[:, :a_w] * qa_scale, w_in[:, a_w:3 * a_w]], axis=1).astype(BF16)
    c_cols = MLA_Q_LORA + MLA_KV_LORA + LANES
    w_c = _pad_cols(w_in[:, 3 * a_w:], _round_up(c_cols, 512)).astype(BF16)
    qkv = matmul(hb, w_qkv, BF16)
    cproj = matmul(hb, w_c, F32)

    oa = band_attention(qkv.reshape(batch, seq, 3 * a_w), band_bias_tiles(rel_bias))

    qh = MLA_NOPE + MLA_ROPE
    w_uq_p = jnp.pad(w_uq.reshape(MLA_Q_LORA, B_HEADS, qh), ((0, 0), (0, 0), (0, 2 * LANES - qh)))
    w_uq_p = w_uq_p.reshape(MLA_Q_LORA, B_HEADS * 2 * LANES).astype(BF16)
    cq_n = rms_norm_cols(cproj, 0, MLA_Q_LORA, q_norm)
    ckv_n = rms_norm_cols(cproj, MLA_Q_LORA // MLA_KV_LORA, MLA_KV_LORA, kv_norm)
    q_lat = matmul(cq_n, w_uq_p, F32)
    q_scale = (MLA_NOPE + MLA_ROPE) ** -0.5 * LOG2E
    q_cat = rope_cols(q_lat, 0, B_HEADS, 2 * LANES, MLA_ROPE // 2,
                      rope_tables(seq, MLA_ROPE, 2 * LANES, offset=MLA_NOPE, scale=q_scale), seq)
    kv = matmul(ckv_n, w_ukv.astype(BF16), BF16)
    k_cat, v_cat = mla_assemble_kv(kv, cproj, (MLA_Q_LORA + MLA_KV_LORA) // LANES,
                                   rope_tables(seq, MLA_ROPE, LANES), seq)
    ob = flash_attention(q_cat.reshape(batch, seq, -1), k_cat.reshape(batch, seq, -1), v_cat.reshape(batch, seq, -1),
                         n_heads=B_HEADS, hq=4, hk=4, dk=2 * LANES, dv=MLA_V, q_blk=0, k_blk=0, v_blk=0,
                         mode="chunk")
    return matmul_cat(oa.reshape(T, -1), ob.reshape(T, -1), w_out.astype(BF16), F32)


def mixer_cd(hb, batch, seq, w_in, forget_bias, w_out):
    T, D = hb.shape
    c_qw, c_kw, d_w = C_HEADS * HEAD_DIM, C_KV_HEADS * HEAD_DIM, D_HEADS * HEAD_DIM
    i_w = IDX_HEADS * IDX_HD
    offs = np.cumsum([0, c_qw, c_kw, c_kw, i_w, IDX_HD, IDX_HEADS, d_w, d_w, d_w, D_HEADS])
    col = lambda n: w_in[:, offs[n]:offs[n + 1]]
    q_scale = HEAD_DIM ** -0.5 * LOG2E
    w_a = jnp.concatenate([col(0) * q_scale, col(1), col(2), col(6) * q_scale, col(7), col(8)], axis=1).astype(BF16)
    w_b = jnp.concatenate([col(3), _pad_cols(col(4), LANES), _pad_cols(col(5), LANES), _pad_cols(col(9), LANES)], axis=1)
    w_b = _pad_cols(w_b, _round_up(w_b.shape[1], 512)).astype(BF16)
    proj_a = matmul(hb, w_a, BF16)
    proj_b = matmul(hb, w_b, F32)
    qk_w = c_qw + c_kw
    blk_ki, blk_wi, blk_fd = i_w // LANES, i_w // LANES + 1, i_w // LANES + 2

    qk_rot = rope_cols(proj_a, 0, qk_w // LANES, LANES, HEAD_DIM // 2, rope_tables(seq, HEAD_DIM, LANES), seq)
    qi_rot = rope_cols(proj_b, 0, i_w // LANES, LANES, IDX_HD // 2, rope_tables(seq, IDX_HD, LANES, reps=2), seq)
    ki_rot = rope_cols(proj_b, blk_ki, 1, LANES, IDX_HD // 2, rope_tables(seq, IDX_HD, LANES), seq)
    nq = seq // IDX_QB
    qit = qi_rot.reshape(batch, nq, IDX_QB, IDX_HEADS, IDX_HD).transpose(0, 1, 4, 3, 2)
    qit = qit.reshape(batch, nq, IDX_HD, IDX_HEADS * IDX_QB)
    wi = proj_b[:, blk_wi * LANES:blk_wi * LANES + IDX_HEADS]
    wt = wi.reshape(batch, nq, IDX_QB, IDX_HEADS).transpose(0, 1, 3, 2).reshape(batch, nq, 1, IDX_HEADS * IDX_QB)
    sel_bias = dsa_selection_bias(ki_rot.reshape(batch, seq, LANES), qit, wt, min(DSA_TOPK_MAX, seq // 4))
    qk3 = qk_rot.reshape(batch, seq, qk_w)
    pa3 = proj_a.reshape(batch, seq, -1)
    rep = C_HEADS // C_KV_HEADS
    oc = flash_attention(qk3, qk3, pa3, n_heads=C_HEADS, hq=rep, hk=1, dk=HEAD_DIM, dv=HEAD_DIM,
                         q_blk=0, k_blk=c_qw // HEAD_DIM, v_blk=qk_w // HEAD_DIM, mode="bias", bias=sel_bias)

    fbias = _pad_cols(forget_bias.reshape(1, D_HEADS).astype(F32), LANES)
    cum = fox_cumsum(proj_b, blk_fd, fbias, batch, seq)[:, :, :D_HEADS]
    cum_t = cum.transpose(0, 2, 1)
    hd = 4
    d0 = (qk_w + c_kw) // (hd * HEAD_DIM)
    od = flash_attention(pa3, pa3, pa3, n_heads=D_HEADS, hq=hd, hk=hd, dk=HEAD_DIM, dv=HEAD_DIM,
                         q_blk=d0, k_blk=d0 + d_w // (hd * HEAD_DIM), v_blk=d0 + 2 * d_w // (hd * HEAD_DIM),
                         mode="causal", fox=(cum_t[..., None], cum_t[:, :, None, :]))
    return matmul_cat(oc.reshape(T, -1), od.reshape(T, -1), w_out.astype(BF16), F32)


def memory_cross_attention(h, hb, mem_b, batch, seq, w_q, w_kv, w_o, ln_g, ln_b):
    T, D = hb.shape
    mem_w = MEM_HEADS * MEM_HD
    q_scale = MEM_HD ** -0.5 * LOG2E
    q = matmul(hb, (w_q * q_scale).astype(BF16), BF16)
    kv = matmul(mem_b.reshape(-1, D), w_kv.astype(BF16), BF16)
    kv3 = kv.reshape(batch, -1, 2 * mem_w)
    o = flash_attention(q.reshape(batch, seq, mem_w), kv3, kv3, n_heads=MEM_HEADS, hq=MEM_HEADS, hk=MEM_HEADS,
                        dk=MEM_HD, dv=MEM_HD, q_blk=0, k_blk=0, v_blk=1, mode="none", kb=kv3.shape[1])
    return matmul_deepnorm_ln(o.reshape(T, mem_w), w_o.astype(BF16), h, ln_g, ln_b)


def peer_ffn(ht, w_q, sub_keys, u_tabs, v_tabs, layer):
    qt = matmul(w_q.T.astype(BF16), ht, F32)
    sk = sub_keys.reshape(2 * PEER_HEADS, N_KEYS, PEER_DKEY // 2).astype(BF16)
    s_t, stats = peer_route(qt, sk)
    u_b, v_b = cast_layer_pair(u_tabs, v_tabs, layer)
    return peer_dense(ht, u_b, v_b, s_t, stats)


def kernel(x, mem, ab_w_in, a_rel_bias, b_q_norm, b_w_uq, b_kv_norm, b_w_ukv, ab_w_out, cd_w_in, d_forget_bias,
           cd_w_out, mem_w_q, mem_w_kv, mem_w_o, peer_w_q, peer_sub_keys, peer_u, peer_v, ln_g, ln_b):
    batch, seq, d_model = x.shape
    h = x.reshape(batch * seq, d_model)
    hb = h.astype(BF16)
    mem_b = mem.astype(BF16)
    for layer in range(DEPTH):
        j = layer // 2
        if layer % 2 == 0:
            y = mixer_ab(hb, batch, seq, ab_w_in[j], a_rel_bias[j], b_q_norm[j], b_w_uq[j], b_kv_norm[j],
                         b_w_ukv[j], ab_w_out[j])
        else:
            y = mixer_cd(hb, batch, seq, cd_w_in[j], d_forget_bias[j], cd_w_out[j])
        h, hb = deepnorm_ln(h, y, ln_g[layer, 0], ln_b[layer, 0])
        h, ht = memory_cross_attention(h, hb, mem_b, batch, seq, mem_w_q[layer], mem_w_kv[layer], mem_w_o[layer],
                                       ln_g[layer, 1], ln_b[layer, 1])
        y = peer_ffn(ht, peer_w_q[layer], peer_sub_keys[layer], peer_u, peer_v, layer)
        h, hb = deepnorm_ln(h, y, ln_g[layer, 2], ln_b[layer, 2])
    return h.reshape(batch, seq, d_model)
```

```python
import functools

import jax
import jax.numpy as jnp
import numpy as np
from jax import lax
from jax.experimental import pallas as pl
from jax.experimental.pallas import tpu as pltpu

F32 = jnp.float32
BF16 = jnp.bfloat16

CHUNK = 64
HEAD_DIM = 128
A_HEADS = 16
LEFT_CHUNKS = 8
REL_CLIP = 256
B_HEADS = 16
MLA_Q_LORA = 1536
MLA_KV_LORA = 512
MLA_NOPE = 128
MLA_ROPE = 64
MLA_V = 128
C_HEADS = 16
C_KV_HEADS = 4
IDX_HEADS = 32
IDX_HD = 64
DSA_TOPK_MAX = 256
D_HEADS = 16
MEM_HEADS = 4
MEM_HD = 128
PEER_HEADS = 8
PEER_DKEY = 256
N_KEYS = 128
PEER_TOPK = 16
ROPE_THETA = 10000.0
LN_EPS = 1e-5
RMS_EPS = 1e-6
DEPTH = 2
DEEPNORM_ALPHA = (2 * DEPTH) ** 0.25

LANES = 128
V7X_VMEM_LIMIT = 56 * 1024 * 1024
MASK_NEG = -1e30
LOG2E = 1.4426950408889634


def _params(sem, vmem=V7X_VMEM_LIMIT, flags=None):
    return pltpu.CompilerParams(dimension_semantics=sem, vmem_limit_bytes=vmem, flags=flags)


def _round_up(n, m):
    return (n + m - 1) // m * m


def _mm_kernel(a_ref, w_ref, o_ref):
    o_ref[...] = jnp.dot(a_ref[...], w_ref[...], preferred_element_type=F32).astype(o_ref.dtype)


def matmul(a, w, out_dtype, tm=1024, tn=512):
    M, K = a.shape
    N = w.shape[1]
    tm = min(tm, M)
    tn = min(tn, N)
    assert M % tm == 0 and N % tn == 0, (M, N, tm, tn)
    return pl.pallas_call(
        _mm_kernel,
        grid=(M // tm, N // tn),
        in_specs=[pl.BlockSpec((tm, K), lambda i, j: (i, 0)),
                  pl.BlockSpec((K, tn), lambda i, j: (0, j))],
        out_specs=pl.BlockSpec((tm, tn), lambda i, j: (i, j)),
        out_shape=jax.ShapeDtypeStruct((M, N), out_dtype),
        compiler_params=_params(("parallel", "arbitrary")),
        name="matmul",
    )(a, w)


def _mm2_kernel(a1_ref, a2_ref, w_ref, o_ref):
    a = jnp.concatenate([a1_ref[...], a2_ref[...]], axis=1)
    o_ref[...] = jnp.dot(a, w_ref[...], preferred_element_type=F32).astype(o_ref.dtype)


def matmul_cat(a1, a2, w, out_dtype, tm=1024, tn=512):
    M, K1 = a1.shape
    K2 = a2.shape[1]
    N = w.shape[1]
    tm = min(tm, M)
    tn = min(tn, N)
    assert M % tm == 0 and N % tn == 0 and w.shape[0] == K1 + K2
    return pl.pallas_call(
        _mm2_kernel,
        grid=(M // tm, N // tn),
        in_specs=[pl.BlockSpec((tm, K1), lambda i, j: (i, 0)),
                  pl.BlockSpec((tm, K2), lambda i, j: (i, 0)),
                  pl.BlockSpec((K1 + K2, tn), lambda i, j: (0, j))],
        out_specs=pl.BlockSpec((tm, tn), lambda i, j: (i, j)),
        out_shape=jax.ShapeDtypeStruct((M, N), out_dtype),
        compiler_params=_params(("parallel", "arbitrary")),
        name="matmul_cat",
    )(a1, a2, w)


def _cast_kernel(u_ref, v_ref, ou_ref, ov_ref):
    ou_ref[...] = u_ref[...].astype(ou_ref.dtype)
    ov_ref[...] = v_ref[...].astype(ov_ref.dtype)


def cast_layer_pair(u, v, layer, tr=256):
    _, R, C = u.shape
    src = pl.BlockSpec((None, tr, C), lambda i: (layer, i, 0))
    dst = pl.BlockSpec((tr, C), lambda i: (i, 0))
    return pl.pallas_call(
        _cast_kernel,
        grid=(R // tr,),
        in_specs=[src, src],
        out_specs=[dst, dst],
        out_shape=[jax.ShapeDtypeStruct((R, C), BF16)] * 2,
        compiler_params=_params(("parallel",)),
        name="cast_layer_pair",
    )(u, v)


def _ln_kernel(h_ref, y_ref, g_ref, b_ref, o_ref, ob_ref):
    z = DEEPNORM_ALPHA * h_ref[...] + y_ref[...]
    zc = z - jnp.mean(z, axis=-1, keepdims=True)
    var = jnp.mean(zc * zc, axis=-1, keepdims=True)
    out = zc * lax.rsqrt(var + LN_EPS) * g_ref[...] + b_ref[...]
    o_ref[...] = out
    ob_ref[...] = out.astype(BF16)


def deepnorm_ln(h, y, g, b, tb=256):
    T, D = h.shape
    row = pl.BlockSpec((tb, D), lambda i: (i, 0))
    vec = pl.BlockSpec((1, D), lambda i: (0, 0))
    return pl.pallas_call(
        _ln_kernel,
        grid=(T // tb,),
        in_specs=[row, row, vec, vec],
        out_specs=[row, row],
        out_shape=[jax.ShapeDtypeStruct((T, D), F32), jax.ShapeDtypeStruct((T, D), BF16)],
        compiler_params=_params(("parallel",)),
        name="deepnorm_ln",
    )(h, y, g.reshape(1, D), b.reshape(1, D))


def _mm_ln_kernel(a_ref, w_ref, h_ref, g_ref, b_ref, o_ref, obt_ref):
    z = DEEPNORM_ALPHA * h_ref[...] + jnp.dot(a_ref[...], w_ref[...], preferred_element_type=F32)
    zc = z - jnp.mean(z, axis=-1, keepdims=True)
    var = jnp.mean(zc * zc, axis=-1, keepdims=True)
    out = zc * lax.rsqrt(var + LN_EPS) * g_ref[...] + b_ref[...]
    o_ref[...] = out
    obt_ref[...] = out.T.astype(BF16)


def matmul_deepnorm_ln(a, w, h, g, b, tb=256):
    T, D = h.shape
    K = a.shape[1]
    row = pl.BlockSpec((tb, D), lambda i: (i, 0))
    vec = pl.BlockSpec((1, D), lambda i: (0, 0))
    return pl.pallas_call(
        _mm_ln_kernel,
        grid=(T // tb,),
        in_specs=[pl.BlockSpec((tb, K), lambda i: (i, 0)), pl.BlockSpec((K, D), lambda i: (0, 0)), row, vec, vec],
        out_specs=[row, pl.BlockSpec((D, tb), lambda i: (0, i))],
        out_shape=[jax.ShapeDtypeStruct((T, D), F32), jax.ShapeDtypeStruct((D, T), BF16)],
        compiler_params=_params(("parallel",)),
        name="matmul_deepnorm_ln",
    )(a, w, h, g.reshape(1, D), b.reshape(1, D))


def _rms_kernel(x_ref, g_ref, o_ref):
    x = x_ref[...].astype(F32)
    o_ref[...] = (x * lax.rsqrt(jnp.mean(x * x, axis=-1, keepdims=True) + RMS_EPS) * g_ref[...]).astype(o_ref.dtype)


def rms_norm_cols(x, col_block, width, g, tb=512):
    T = x.shape[0]
    return pl.pallas_call(
        _rms_kernel,
        grid=(T // tb,),
        in_specs=[pl.BlockSpec((tb, width), lambda i: (i, col_block)),
                  pl.BlockSpec((1, width), lambda i: (0, 0))],
        out_specs=pl.BlockSpec((tb, width), lambda i: (i, 0)),
        out_shape=jax.ShapeDtypeStruct((T, width), BF16),
        compiler_params=_params(("parallel",)),
        name="rms_norm",
    )(x, g.reshape(1, width))


def rope_tables(seq, dim, group, offset=0, reps=1, scale=1.0):
    half = dim // 2
    inv = ROPE_THETA ** (-jnp.arange(half, dtype=F32) * 2.0 / dim)
    ang = jnp.arange(seq, dtype=F32)[:, None] * inv[None, :]
    cos, sin = jnp.cos(ang), jnp.sin(ang)
    zeros_tail = jnp.zeros((seq, group - offset - dim * reps), F32)
    head = jnp.ones((seq, offset), F32)
    zhead = jnp.zeros((seq, offset), F32)
    zhalf = jnp.zeros((seq, half), F32)
    c = jnp.concatenate([head] + [cos, cos] * reps + [zeros_tail], axis=1)
    sa = jnp.concatenate([zhead] + [-sin, zhalf] * reps + [zeros_tail], axis=1)
    sb = jnp.concatenate([zhead] + [zhalf, sin] * reps + [zeros_tail], axis=1)
    return c * scale, sa * scale, sb * scale


def _rope_kernel(x_ref, c_ref, sa_ref, sb_ref, o_ref, *, groups, group, half):
    c, sa, sb = c_ref[...], sa_ref[...], sb_ref[...]
    for g in range(groups):
        x = x_ref[:, g * group:(g + 1) * group].astype(F32)
        out = x * c + pltpu.roll(x, group - half, 1) * sa + pltpu.roll(x, half, 1) * sb
        o_ref[:, g * group:(g + 1) * group] = out.astype(o_ref.dtype)


def rope_cols(x, col_block, groups, group, half, tables, seq, tb=256):
    T = x.shape[0]
    W = groups * group
    nsb = seq // tb
    tab = pl.BlockSpec((tb, group), lambda i: (i % nsb, 0))
    return pl.pallas_call(
        functools.partial(_rope_kernel, groups=groups, group=group, half=half),
        grid=(T // tb,),
        in_specs=[pl.BlockSpec((tb, W), lambda i: (i, col_block)), tab, tab, tab],
        out_specs=pl.BlockSpec((tb, W), lambda i: (i, 0)),
        out_shape=jax.ShapeDtypeStruct((T, W), BF16),
        compiler_params=_params(("parallel",)),
        name="rope",
    )(x, *tables)


def _mla_kv_kernel(kv_ref, kr_ref, c_ref, sa_ref, sb_ref, k_ref, v_ref, *, heads, half):
    kr = kr_ref[...].astype(F32)
    kr = kr * c_ref[...] + pltpu.roll(kr, LANES - half, 1) * sa_ref[...] + pltpu.roll(kr, half, 1) * sb_ref[...]
    kr = kr.astype(k_ref.dtype)
    for h in range(heads):
        k_ref[:, 2 * h * LANES:(2 * h + 1) * LANES] = kv_ref[:, 2 * h * LANES:(2 * h + 1) * LANES]
        k_ref[:, (2 * h + 1) * LANES:(2 * h + 2) * LANES] = kr
        v_ref[:, h * LANES:(h + 1) * LANES] = kv_ref[:, (2 * h + 1) * LANES:(2 * h + 2) * LANES]


def mla_assemble_kv(kv, cproj, kr_col_block, tables, seq, tb=256):
    T, W = kv.shape
    nsb = seq // tb
    tab = pl.BlockSpec((tb, LANES), lambda i: (i % nsb, 0))
    return pl.pallas_call(
        functools.partial(_mla_kv_kernel, heads=B_HEADS, half=MLA_ROPE // 2),
        grid=(T // tb,),
        in_specs=[pl.BlockSpec((tb, W), lambda i: (i, 0)),
                  pl.BlockSpec((tb, LANES), lambda i: (i, kr_col_block)), tab, tab, tab],
        out_specs=[pl.BlockSpec((tb, W), lambda i: (i, 0)), pl.BlockSpec((tb, W // 2), lambda i: (i, 0))],
        out_shape=[jax.ShapeDtypeStruct((T, W), BF16), jax.ShapeDtypeStruct((T, W // 2), BF16)],
        compiler_params=_params(("parallel",)),
        name="mla_assemble_kv",
    )(kv, cproj, *tables)


FLASH_RB = 128


def _flash_kernel(*refs, hq, hk, dk, dv, qb, kb, mode, fox, q_axis, n_kv_blocks):
    it = iter(refs)
    q_ref, k_ref, v_ref = next(it), next(it), next(it)
    ct_ref = cs_ref = bias_ref = None
    if fox:
        ct_ref, cs_ref = next(it), next(it)
    if mode == "bias":
        bias_ref = next(it)
    o_ref, m_ref, l_ref, acc_ref, p_ref, alpha_ref = next(it), next(it), next(it), next(it), next(it), next(it)
    ct_lanes_ref = next(it) if fox else None

    qi = pl.program_id(q_axis)
    m_ref[...] = jnp.full(m_ref.shape, MASK_NEG, F32)
    l_ref[...] = jnp.zeros(l_ref.shape, F32)
    acc_ref[...] = jnp.zeros(acc_ref.shape, F32)
    if fox:
        for h in range(hq):
            ct_lanes_ref[h] = jnp.broadcast_to(ct_ref[0, h] * LOG2E, (qb, LANES))

    n_sub = qb // FLASH_RB
    per_kv = kb // FLASH_RB
    rows = lax.broadcasted_iota(jnp.int32, (FLASH_RB, kb), 0)
    cols = lax.broadcasted_iota(jnp.int32, (FLASH_RB, kb), 1)

    def diag_mask(r):
        row_in_kv = rows + (r % per_kv) * FLASH_RB
        if mode == "causal":
            return cols <= row_in_kv
        if mode == "chunk":
            return (cols // CHUNK) <= (row_in_kv // CHUNK)
        return None

    def step(j, visible):
        ks = pl.multiple_of(j * kb, kb)
        for h in range(hq):
            g = h if hk == hq else 0
            k = k_ref[0, pl.ds(ks, kb), g * dk:(g + 1) * dk]
            v = v_ref[0, pl.ds(ks, kb), g * dv:(g + 1) * dv]
            s_all = lax.dot_general(q_ref[0, :, h * dk:(h + 1) * dk], k, (((1,), (1,)), ((), ())),
                                    preferred_element_type=F32)
            cs = cs_ref[0, h, :, pl.ds(ks, kb)] * LOG2E if fox else None
            for r in range(n_sub):
                rs = slice(r * FLASH_RB, (r + 1) * FLASH_RB)
                if visible[r] == "none":
                    p_ref[h, rs, :] = jnp.zeros((FLASH_RB, kb), p_ref.dtype)
                    alpha_ref[h, rs, :] = jnp.ones((FLASH_RB, LANES), F32)
                    continue
                s = s_all[rs]
                if bias_ref is not None:
                    s = s + bias_ref[0, rs, pl.ds(ks, kb)].astype(F32)
                blocks = [s[:, c * LANES:(c + 1) * LANES] for c in range(kb // LANES)]
                if fox:
                    ct = ct_lanes_ref[h, rs]
                    blocks = [blk + ct - cs[:, c * LANES:(c + 1) * LANES] for c, blk in enumerate(blocks)]
                if visible[r] == "diag" and mode in ("causal", "chunk"):
                    mask = diag_mask(r)
                    blocks = [jnp.where(mask[:, c * LANES:(c + 1) * LANES], blk, MASK_NEG)
                              for c, blk in enumerate(blocks)]
                blk_max = blocks[0]
                for blk in blocks[1:]:
                    blk_max = jnp.maximum(blk_max, blk)
                m_prev = m_ref[h, rs]
                m_new = jnp.maximum(m_prev, jnp.max(blk_max, axis=1, keepdims=True))
                alpha = jnp.exp2(m_prev - m_new)
                p_blocks = [jnp.exp2(blk - m_new) for blk in blocks]
                p_lanes = p_blocks[0]
                for pb in p_blocks[1:]:
                    p_lanes = p_lanes + pb
                l_ref[h, rs] = alpha * l_ref[h, rs] + p_lanes
                for c, pb in enumerate(p_blocks):
                    p_ref[h, rs, c * LANES:(c + 1) * LANES] = pb.astype(p_ref.dtype)
                alpha_ref[h, rs, :] = alpha
                m_ref[h, rs] = m_new
            pv = jnp.dot(p_ref[h], v, preferred_element_type=F32)
            acc_ref[h] = alpha_ref[h] * acc_ref[h] + pv

    def body(j, carry):
        step(j, ["full"] * n_sub)
        return carry

    if mode == "none":
        lax.fori_loop(0, n_kv_blocks, body, 0)
    else:
        kv_per_q = qb // kb
        lax.fori_loop(0, qi * kv_per_q, body, 0)
        for t in range(kv_per_q):
            kinds = ["full" if r // per_kv > t else "diag" if r // per_kv == t else "none" for r in range(n_sub)]
            step(qi * kv_per_q + t, kinds)

    for h in range(hq):
        l = jnp.sum(l_ref[h], axis=1, keepdims=True)
        o_ref[0, :, h * dv:(h + 1) * dv] = (acc_ref[h] / l).astype(o_ref.dtype)


def flash_attention(q, k, v, *, n_heads, hq, hk, dk, dv, q_blk, k_blk, v_blk, mode,
                    qb=512, kb=512, fox=None, bias=None):
    B, Sq = q.shape[0], q.shape[1]
    Skv = k.shape[1]
    kb = min(kb, Skv)
    nq, ng = Sq // qb, n_heads // hq
    assert hk in (hq, 1) and qb % kb == 0 and kb % FLASH_RB == 0
    if mode in ("causal", "chunk", "bias"):
        assert Sq == Skv
    if mode != "bias":
        grid = (B, ng, nq)
        q_axis = 2

        def spec(shape, fn):
            return pl.BlockSpec(shape, lambda b, g, i: fn(b, g, i))
    else:
        grid = (B, nq, ng)
        q_axis = 1

        def spec(shape, fn):
            return pl.BlockSpec(shape, lambda b, i, g: fn(b, g, i))

    in_specs = [
        spec((1, qb, hq * dk), lambda b, g, i: (b, i, q_blk + g)),
        spec((1, Skv, hk * dk), lambda b, g, i: (b, 0, k_blk + g)),
        spec((1, Skv, hk * dv), lambda b, g, i: (b, 0, v_blk + g)),
    ]
    args = [q, k, v]
    if fox is not None:
        ct, cs = fox
        in_specs += [spec((1, hq, qb, 1), lambda b, g, i: (b, g, i, 0)),
                     spec((1, hq, 1, Skv), lambda b, g, i: (b, g, 0, 0))]
        args += [ct, cs]
    if mode == "bias":
        in_specs.append(spec((1, qb, Skv), lambda b, g, i: (b, i, 0)))
        args.append(bias)
    assert dv == LANES
    stat = pltpu.VMEM((hq, qb, LANES), F32)
    kern = functools.partial(_flash_kernel, hq=hq, hk=hk, dk=dk, dv=dv, qb=qb, kb=kb, mode=mode,
                             fox=fox is not None, q_axis=q_axis, n_kv_blocks=Skv // kb)
    return pl.pallas_call(
        kern,
        grid=grid,
        in_specs=in_specs,
        out_specs=spec((1, qb, hq * dv), lambda b, g, i: (b, i, g)),
        out_shape=jax.ShapeDtypeStruct((B, Sq, n_heads * dv), BF16),
        scratch_shapes=[stat, stat, pltpu.VMEM((hq, qb, dv), F32), pltpu.VMEM((hq, qb, kb), BF16), stat]
        + ([stat] if fox is not None else []),
        compiler_params=_params(("parallel", "parallel", "arbitrary")),
        name="flash_" + mode,
    )(*args)


BAND_QB = 4 * CHUNK
BAND_KBLOCKS = (LEFT_CHUNKS * CHUNK) // BAND_QB + 1
BAND_W = BAND_KBLOCKS * BAND_QB


def band_bias_tiles(rel_bias):
    pad = LEFT_CHUNKS * CHUNK
    period = BAND_QB + BAND_W
    m = np.arange(period)
    j_minus_i = np.where(m < period - BAND_QB, m, m - period)
    r = rel_bias[:, np.clip(pad - j_minus_i, -REL_CLIP, REL_CLIP) + REL_CLIP].astype(F32)
    skew = jnp.tile(r, (1, BAND_QB))[:, :BAND_QB * (period - 1)].reshape(-1, BAND_QB, period - 1)
    i = np.arange(BAND_QB)[:, None]
    j = np.arange(BAND_W)[None, :]
    chunk_diff = i // CHUNK + LEFT_CHUNKS - j // CHUNK
    in_band = (chunk_diff >= 0) & (chunk_diff <= LEFT_CHUNKS)
    return jnp.where(in_band[None], skew[:, :, :BAND_W] * LOG2E, MASK_NEG)


def _band_kernel(q_ref, k_ref, v_ref, bias_ref, o_ref, s_ref, p_ref, l_ref, *, hb):
    qi = pl.program_id(2)
    for h in range(hb):
        lanes = slice(h * HEAD_DIM, (h + 1) * HEAD_DIM)
        q = q_ref[0, :, lanes]
        v_parts = []
        for jb in range(BAND_KBLOCKS):
            kblk = qi - (BAND_KBLOCKS - 1) + jb
            ks = pl.multiple_of(jnp.maximum(kblk, 0) * BAND_QB, BAND_QB)
            k = k_ref[0, pl.ds(ks, BAND_QB), lanes]
            v_parts.append(v_ref[0, pl.ds(ks, BAND_QB), lanes])
            s = lax.dot_general(q, k, (((1,), (1,)), ((), ())), preferred_element_type=F32)
            s = s + bias_ref[h, :, jb * BAND_QB:(jb + 1) * BAND_QB]
            s_ref[h, :, jb * BAND_QB:(jb + 1) * BAND_QB] = jnp.where(kblk >= 0, s, MASK_NEG)
        for r in range(BAND_QB // CHUNK):
            rs = slice(r * CHUNK, (r + 1) * CHUNK)
            blocks = [s_ref[h, rs, c * LANES:(c + 1) * LANES] for c in range(BAND_W // LANES)]
            blk_max = blocks[0]
            for blk in blocks[1:]:
                blk_max = jnp.maximum(blk_max, blk)
            m = jnp.broadcast_to(jnp.max(blk_max, axis=1, keepdims=True), (CHUNK, LANES))
            p_lanes = jnp.zeros((CHUNK, LANES), F32)
            for c, blk in enumerate(blocks):
                p = jnp.exp2(blk - m)
                p_lanes = p_lanes + p
                p_ref[h, rs, c * LANES:(c + 1) * LANES] = p.astype(p_ref.dtype)
            l_ref[h, rs, :] = p_lanes
        acc = jnp.zeros((BAND_QB, HEAD_DIM), F32)
        for jb in range(BAND_KBLOCKS):
            acc = acc + jnp.dot(p_ref[h, :, jb * BAND_QB:(jb + 1) * BAND_QB], v_parts[jb], preferred_element_type=F32)
        l = jnp.sum(l_ref[h], axis=1, keepdims=True)
        o_ref[0, :, lanes] = (acc / l).astype(o_ref.dtype)


def band_attention(qkv, bias_tiles, hb=4):
    B, S, _ = qkv.shape
    H = A_HEADS
    ng = H // hb
    return pl.pallas_call(
        functools.partial(_band_kernel, hb=hb),
        grid=(B, ng, S // BAND_QB),
        in_specs=[pl.BlockSpec((1, BAND_QB, hb * HEAD_DIM), lambda b, g, i: (b, i, g)),
                  pl.BlockSpec((1, S, hb * HEAD_DIM), lambda b, g, i: (b, 0, ng + g)),
                  pl.BlockSpec((1, S, hb * HEAD_DIM), lambda b, g, i: (b, 0, 2 * ng + g)),
                  pl.BlockSpec((hb, BAND_QB, BAND_W), lambda b, g, i: (g, 0, 0))],
        out_specs=pl.BlockSpec((1, BAND_QB, hb * HEAD_DIM), lambda b, g, i: (b, i, g)),
        out_shape=jax.ShapeDtypeStruct((B, S, H * HEAD_DIM), BF16),
        scratch_shapes=[pltpu.VMEM((hb, BAND_QB, BAND_W), F32), pltpu.VMEM((hb, BAND_QB, BAND_W), BF16),
                        pltpu.VMEM((hb, BAND_QB, LANES), F32)],
        compiler_params=_params(("parallel", "parallel", "arbitrary")),
        name="band_attention",
    )(qkv, qkv, qkv, bias_tiles)


def _fox_cumsum_kernel(f_ref, b_ref, o_ref, carry_ref, *, cb):
    @pl.when(pl.program_id(1) == 0)
    def _():
        carry_ref[...] = jnp.zeros(carry_ref.shape, F32)

    x = f_ref[0].astype(F32) + b_ref[...]
    log_f = jnp.minimum(x, 0.0) - jnp.log1p(jnp.exp(-jnp.abs(x)))
    r = lax.broadcasted_iota(jnp.int32, (cb, cb), 0)
    c = lax.broadcasted_iota(jnp.int32, (cb, cb), 1)
    tri = jnp.where(c <= r, 1.0, 0.0).astype(F32)
    cum = jnp.dot(tri, log_f, preferred_element_type=F32, precision=lax.Precision.HIGHEST) + carry_ref[...]
    o_ref[0] = cum
    carry_ref[...] = cum[cb - 1:cb, :]


def fox_cumsum(proj, col_block, bias_row, batch, seq, cb=256):
    x = proj.reshape(batch, seq, proj.shape[-1])
    return pl.pallas_call(
        functools.partial(_fox_cumsum_kernel, cb=cb),
        grid=(batch, seq // cb),
        in_specs=[pl.BlockSpec((1, cb, LANES), lambda b, i: (b, i, col_block)),
                  pl.BlockSpec((1, LANES), lambda b, i: (0, 0))],
        out_specs=pl.BlockSpec((1, cb, LANES), lambda b, i: (b, i, 0)),
        out_shape=jax.ShapeDtypeStruct((batch, seq, LANES), F32),
        scratch_shapes=[pltpu.VMEM((1, LANES), F32)],
        compiler_params=_params(("parallel", "arbitrary")),
        name="fox_cumsum",
    )(x, bias_row)


IDX_QB = 128
IDX_KC = 512


def _sortable_key(x):
    bits = pltpu.bitcast(x, jnp.int32)
    return bits ^ ((bits >> 31) & jnp.int32(0x7FFFFFFF))


def _indexer_kernel(ki_ref, qit_ref, w_ref, o_ref, key_ref, *, seq, topk):
    qi = pl.program_id(1)
    q0 = qi * IDX_QB
    n_chunks = (q0 + IDX_QB + IDX_KC - 1) // IDX_KC
    qchunk = (q0 + lax.broadcasted_iota(jnp.int32, (1, IDX_QB), 1)) // CHUNK
    int_min = jnp.int32(-2 ** 31)
    w_scale = IDX_HEADS ** -0.5 * IDX_HD ** -0.5

    def score_chunk(c, carry):
        ks = pl.multiple_of(c * IDX_KC, IDX_KC)
        ki = ki_ref[0, pl.ds(ks, IDX_KC), :][:, :IDX_HD]
        acc = jnp.zeros((IDX_KC, IDX_QB), F32)
        for hp in range(IDX_HEADS // 2):
            t = jnp.dot(ki, qit_ref[0, 0, :, hp * 2 * IDX_QB:(hp + 1) * 2 * IDX_QB], preferred_element_type=F32)
            w = w_ref[0, 0, :, hp * 2 * IDX_QB:(hp + 1) * 2 * IDX_QB] * w_scale
            t = jnp.maximum(t, 0.0) * w
            acc = acc + t[:, :IDX_QB] + t[:, IDX_QB:]
        kchunk = (ks + lax.broadcasted_iota(jnp.int32, (IDX_KC, 1), 0)) // CHUNK
        acc = jnp.where(kchunk <= qchunk, acc, -jnp.inf)
        key_ref[pl.ds(ks, IDX_KC), :] = _sortable_key(acc)
        return carry

    lax.fori_loop(0, n_chunks, score_chunk, 0)

    def count_ge(cand):
        def body(c, acc):
            ks = pl.multiple_of(c * IDX_KC, IDX_KC)
            blk = key_ref[pl.ds(ks, IDX_KC), :]
            hit = jnp.where(blk >= cand, 1, 0).astype(jnp.int32)
            return acc + jnp.sum(hit.reshape(IDX_KC // 8, 8, IDX_QB), axis=0)
        acc = lax.fori_loop(0, n_chunks, body, jnp.zeros((8, IDX_QB), jnp.int32))
        return jnp.sum(acc, axis=0, keepdims=True)

    def bit_step(i, ans):
        bit = lax.shift_left(jnp.int32(1), 31 - i)
        cand = ans | bit
        cnt = count_ge(cand ^ int_min)
        return jnp.where(cnt >= topk, cand, ans)

    ans = lax.fori_loop(0, 32, bit_step, jnp.zeros((1, IDX_QB), jnp.int32))
    thr = ans ^ int_min
    neg_inf_key = _sortable_key(jnp.full((1, IDX_QB), -jnp.inf, F32))

    def emit(c, carry):
        ks = pl.multiple_of(c * IDX_KC, IDX_KC)
        blk = key_ref[pl.ds(ks, IDX_KC), :]
        sel = (blk >= thr) & (blk > neg_inf_key)
        bias_t = jnp.where(sel, 0.0, MASK_NEG).astype(F32)
        o_ref[0, :, pl.ds(ks, IDX_KC)] = bias_t.T.astype(o_ref.dtype)
        return carry

    lax.fori_loop(0, n_chunks, emit, 0)

    def fill(c, carry):
        ks = pl.multiple_of(c * IDX_KC, IDX_KC)
        o_ref[0, :, pl.ds(ks, IDX_KC)] = jnp.full((IDX_QB, IDX_KC), MASK_NEG, o_ref.dtype)
        return carry

    lax.fori_loop(n_chunks, seq // IDX_KC, fill, 0)


def dsa_selection_bias(ki, qit, wt, topk):
    B, S, _ = ki.shape
    nq = S // IDX_QB
    return pl.pallas_call(
        functools.partial(_indexer_kernel, seq=S, topk=topk),
        grid=(B, nq),
        in_specs=[pl.BlockSpec((1, S, LANES), lambda b, i: (b, 0, 0)),
                  pl.BlockSpec((1, 1, IDX_HD, IDX_HEADS * IDX_QB), lambda b, i: (b, i, 0, 0)),
                  pl.BlockSpec((1, 1, 1, IDX_HEADS * IDX_QB), lambda b, i: (b, i, 0, 0))],
        out_specs=pl.BlockSpec((1, IDX_QB, S), lambda b, i: (b, i, 0)),
        out_shape=jax.ShapeDtypeStruct((B, S, S), BF16),
        scratch_shapes=[pltpu.VMEM((S, IDX_QB), jnp.int32)],
        compiler_params=_params(("parallel", "arbitrary")),
        name="dsa_indexer",
    )(ki, qit, wt)


PEER_TB_ROUTE = 128
STAT_ROWS = 8


def _peer_route_kernel(qt_ref, sk_ref, s_ref, st_ref, top_ref):
    half = PEER_DKEY // 2
    tb = PEER_TB_ROUTE
    for hp in range(2 * PEER_HEADS):
        q = qt_ref[hp * half:(hp + 1) * half, :].astype(BF16)
        s_ref[hp] = jnp.dot(sk_ref[hp], q, preferred_element_type=F32)

    def head_tops(h, carry):
        def extract_pair(i, c):
            xa, xb = c
            ma = jnp.max(xa, axis=0, keepdims=True)
            mb = jnp.max(xb, axis=0, keepdims=True)
            top_ref[2 * h, pl.ds(i, 1), :] = ma
            top_ref[2 * h + 1, pl.ds(i, 1), :] = mb
            return jnp.where(xa == ma, -jnp.inf, xa), jnp.where(xb == mb, -jnp.inf, xb)

        lax.fori_loop(0, PEER_TOPK, extract_pair, (s_ref[2 * h], s_ref[2 * h + 1]))
        return carry

    lax.fori_loop(0, PEER_HEADS, head_tops, 0)

    def candidates(h):
        a = top_ref[2 * h]
        b = top_ref[2 * h + 1]
        return jnp.concatenate([a[0:1, :] + b] + [a[i:i + 1, :] + b[0:8, :] for i in range(1, 8)]
                               + [a[8:PEER_TOPK, :] + b[0:1, :]], axis=0)

    def extract(i, x, mx, z):
        m = jnp.max(x, axis=0, keepdims=True)
        mx = jnp.where(i == 0, m, mx)
        return jnp.where(x == m, -jnp.inf, x), mx, z + jnp.exp(m - mx), m

    def pair_stats(hh, carry):
        h0, h1 = 2 * hh, 2 * hh + 1

        def extract2(i, c):
            x0, mx0, z0, _, x1, mx1, z1, _ = c
            return extract(i, x0, mx0, z0) + extract(i, x1, mx1, z1)

        zero = jnp.zeros((1, tb), F32)
        res = lax.fori_loop(0, PEER_TOPK, extract2, (candidates(h0), zero, zero, zero, candidates(h1), zero, zero, zero))
        for h, (_, _, z, last) in ((h0, res[:4]), (h1, res[4:])):
            st_ref[h, 0:1, :] = last
            st_ref[h, 1:2, :] = top_ref[2 * h, 0:1, :]
            st_ref[h, 2:3, :] = top_ref[2 * h + 1, 0:1, :]
            st_ref[h, 3:4, :] = 1.0 / z
            st_ref[h, 4:STAT_ROWS, :] = jnp.zeros((STAT_ROWS - 4, tb), F32)
        return carry

    lax.fori_loop(0, PEER_HEADS // 2, pair_stats, 0)


def peer_route(qt, sub_keys):
    R, T = qt.shape
    tb = PEER_TB_ROUTE
    nsub = 2 * PEER_HEADS
    return pl.pallas_call(
        _peer_route_kernel,
        grid=(T // tb,),
        in_specs=[pl.BlockSpec((R, tb), lambda i: (0, i)),
                  pl.BlockSpec((nsub, N_KEYS, PEER_DKEY // 2), lambda i: (0, 0, 0))],
        out_specs=[pl.BlockSpec((nsub, N_KEYS, tb), lambda i: (0, 0, i)),
                   pl.BlockSpec((PEER_HEADS, STAT_ROWS, tb), lambda i: (0, 0, i))],
        out_shape=[jax.ShapeDtypeStruct((nsub, N_KEYS, T), F32),
                   jax.ShapeDtypeStruct((PEER_HEADS, STAT_ROWS, T), F32)],
        scratch_shapes=[pltpu.VMEM((nsub, PEER_TOPK, tb), F32)],
        compiler_params=_params(("parallel",)),
        name="peer_route",
    )(qt, sub_keys)


PEER_TB = 512
PEER_EB = 512


def _gelu_exact(x):
    return 0.5 * x * (1.0 + lax.erf(x * (2.0 ** -0.5)))


def _peer_dense_kernel(xt_ref, u_ref, v_ref, s_ref, st_ref, y_ref, e2_ref, ht_ref):
    e = pl.program_id(1)

    @pl.when(e == 0)
    def _():
        y_ref[...] = jnp.zeros(y_ref.shape, F32)
        for h in range(PEER_HEADS):
            e2_ref[h] = jnp.exp(s_ref[2 * h + 1] - st_ref[h, 2:3, :])

    act = _gelu_exact(jnp.dot(u_ref[...], xt_ref[...], preferred_element_type=F32))
    rows_per_step = PEER_EB // N_KEYS
    for r in range(rows_per_step):
        i1 = e * rows_per_step + r
        gate = jnp.zeros((N_KEYS, PEER_TB), F32)
        for h in range(PEER_HEADS):
            s1 = s_ref[2 * h, pl.ds(i1, 1), :]
            e1 = jnp.exp(s1 - st_ref[h, 1:2, :]) * st_ref[h, 3:4, :]
            pair = s_ref[2 * h + 1] + s1
            gate = gate + jnp.where(pair >= st_ref[h, 0:1, :], e2_ref[h] * e1, 0.0)
        ht_ref[r * N_KEYS:(r + 1) * N_KEYS, :] = (gate * act[r * N_KEYS:(r + 1) * N_KEYS, :]).astype(BF16)
    y_ref[...] += lax.dot_general(ht_ref[...], v_ref[...], (((0,), (0,)), ((), ())), preferred_element_type=F32)


def peer_dense(xt, u, v, s_t, stats):
    D, T = xt.shape
    E = u.shape[0]
    nsub = 2 * PEER_HEADS
    once = pl.Buffered(1)
    return pl.pallas_call(
        _peer_dense_kernel,
        grid=(T // PEER_TB, E // PEER_EB),
        in_specs=[pl.BlockSpec((D, PEER_TB), lambda i, e: (0, i), pipeline_mode=once),
                  pl.BlockSpec((PEER_EB, D), lambda i, e: (e, 0)),
                  pl.BlockSpec((PEER_EB, D), lambda i, e: (e, 0)),
                  pl.BlockSpec((nsub, N_KEYS, PEER_TB), lambda i, e: (0, 0, i), pipeline_mode=once),
                  pl.BlockSpec((PEER_HEADS, STAT_ROWS, PEER_TB), lambda i, e: (0, 0, i))],
        out_specs=pl.BlockSpec((PEER_TB, D), lambda i, e: (i, 0)),
        out_shape=jax.ShapeDtypeStruct((T, D), F32),
        scratch_shapes=[pltpu.VMEM((PEER_HEADS, N_KEYS, PEER_TB), F32), pltpu.VMEM((PEER_EB, PEER_TB), BF16)],
        compiler_params=_params(("parallel", "arbitrary")),
        name="peer_dense",
    )(xt, u, v, s_t, stats)


def _pad_cols(w, width):
    return jnp.pad(w, ((0, 0), (0, width - w.shape[1])))


def mixer_ab(hb, batch, seq, w_in, rel_bias, q_norm, w_uq, kv_norm, w_ukv, w_out):
    T, D = hb.shape
    a_w = A_HEADS * HEAD_DIM
    qa_scale = HEAD_DIM ** -0.5 * LOG2E
    w_qkv = jnp.concatenate([w_in[:, :a_w] * qa_scale, w_in[:, a_w:3 * a_w]], axis=1).astype(BF16)
    c_cols = MLA_Q_LORA + MLA_KV_LORA + LANES
    w_c = _pad_cols(w_in[:, 3 * a_w:], _round_up(c_cols, 512)).astype(BF16)
    qkv = matmul(hb, w_qkv, BF16)
    cproj = matmul(hb, w_c, F32)

    oa = band_attention(qkv.reshape(batch, seq, 3 * a_w), band_bias_tiles(rel_bias))

    qh = MLA_NOPE + MLA_ROPE
    w_uq_p = jnp.pad(w_uq.reshape(MLA_Q_LORA, B_HEADS, qh), ((0, 0), (0, 0), (0, 2 * LANES - qh)))
    w_uq_p = w_uq_p.reshape(MLA_Q_LORA, B_HEADS * 2 * LANES).astype(BF16)
    cq_n = rms_norm_cols(cproj, 0, MLA_Q_LORA, q_norm)
    ckv_n = rms_norm_cols(cproj, MLA_Q_LORA // MLA_KV_LORA, MLA_KV_LORA, kv_norm)
    q_lat = matmul(cq_n, w_uq_p, F32)
    q_scale = (MLA_NOPE + MLA_ROPE) ** -0.5 * LOG2E
    q_cat = rope_cols(q_lat, 0, B_HEADS, 2 * LANES, MLA_ROPE // 2,
                      rope_tables(seq, MLA_ROPE, 2 * LANES, offset=MLA_NOPE, scale=q_scale), seq)
    kv = matmul(ckv_n, w_ukv.astype(BF16), BF16)
    k_cat, v_cat = mla_assemble_kv(kv, cproj, (MLA_Q_LORA + MLA_KV_LORA) // LANES,
                                   rope_tables(seq, MLA_ROPE, LANES), seq)
    ob = flash_attention(q_cat.reshape(batch, seq, -1), k_cat.reshape(batch, seq, -1), v_cat.reshape(batch, seq, -1),
                         n_heads=B_HEADS, hq=4, hk=4, dk=2 * LANES, dv=MLA_V, q_blk=0, k_blk=0, v_blk=0,
                         mode="chunk")
    return matmul_cat(oa.reshape(T, -1), ob.reshape(T, -1), w_out.astype(BF16), F32)


def mixer_cd(hb, batch, seq, w_in, forget_bias, w_out):
    T, D = hb.shape
    c_qw, c_kw, d_w = C_HEADS * HEAD_DIM, C_KV_HEADS * HEAD_DIM, D_HEADS * HEAD_DIM
    i_w = IDX_HEADS * IDX_HD
    offs = np.cumsum([0, c_qw, c_kw, c_kw, i_w, IDX_HD, IDX_HEADS, d_w, d_w, d_w, D_HEADS])
    col = lambda n: w_in[:, offs[n]:offs[n + 1]]
    q_scale = HEAD_DIM ** -0.5 * LOG2E
    w_a = jnp.concatenate([col(0) * q_scale, col(1), col(2), col(6) * q_scale, col(7), col(8)], axis=1).astype(BF16)
    w_b = jnp.concatenate([col(3), _pad_cols(col(4), LANES), _pad_cols(col(5), LANES), _pad_cols(col(9), LANES)], axis=1)
    w_b = _pad_cols(w_b, _round_up(w_b.shape[1], 512)).astype(BF16)
    proj_a = matmul(hb, w_a, BF16)
    proj_b = matmul(hb, w_b, F32)
    qk_w = c_qw + c_kw
    blk_ki, blk_wi, blk_fd = i_w // LANES, i_w // LANES + 1, i_w // LANES + 2

    qk_rot = rope_cols(proj_a, 0, qk_w // LANES, LANES, HEAD_DIM // 2, rope_tables(seq, HEAD_DIM, LANES), seq)
    qi_rot = rope_cols(proj_b, 0, i_w // LANES, LANES, IDX_HD // 2, rope_tables(seq, IDX_HD, LANES, reps=2), seq)
    ki_rot = rope_cols(proj_b, blk_ki, 1, LANES, IDX_HD // 2, rope_tables(seq, IDX_HD, LANES), seq)
    nq = seq // IDX_QB
    qit = qi_rot.reshape(batch, nq, IDX_QB, IDX_HEADS, IDX_HD).transpose(0, 1, 4, 3, 2)
    qit = qit.reshape(batch, nq, IDX_HD, IDX_HEADS * IDX_QB)
    wi = proj_b[:, blk_wi * LANES:blk_wi * LANES + IDX_HEADS]
    wt = wi.reshape(batch, nq, IDX_QB, IDX_HEADS).transpose(0, 1, 3, 2).reshape(batch, nq, 1, IDX_HEADS * IDX_QB)
    sel_bias = dsa_selection_bias(ki_rot.reshape(batch, seq, LANES), qit, wt, min(DSA_TOPK_MAX, seq // 4))
    qk3 = qk_rot.reshape(batch, seq, qk_w)
    pa3 = proj_a.reshape(batch, seq, -1)
    rep = C_HEADS // C_KV_HEADS
    oc = flash_attention(qk3, qk3, pa3, n_heads=C_HEADS, hq=rep, hk=1, dk=HEAD_DIM, dv=HEAD_DIM,
                         q_blk=0, k_blk=c_qw // HEAD_DIM, v_blk=qk_w // HEAD_DIM, mode="bias", bias=sel_bias)

    fbias = _pad_cols(forget_bias.reshape(1, D_HEADS).astype(F32), LANES)
    cum = fox_cumsum(proj_b, blk_fd, fbias, batch, seq)[:, :, :D_HEADS]
    cum_t = cum.transpose(0, 2, 1)
    hd = 4
    d0 = (qk_w + c_kw) // (hd * HEAD_DIM)
    od = flash_attention(pa3, pa3, pa3, n_heads=D_HEADS, hq=hd, hk=hd, dk=HEAD_DIM, dv=HEAD_DIM,
                         q_blk=d0, k_blk=d0 + d_w // (hd * HEAD_DIM), v_blk=d0 + 2 * d_w // (hd * HEAD_DIM),
                         mode="causal", fox=(cum_t[..., None], cum_t[:, :, None, :]))
    return matmul_cat(oc.reshape(T, -1), od.reshape(T, -1), w_out.astype(BF16), F32)


def memory_cross_attention(h, hb, mem_b, batch, seq, w_q, w_kv, w_o, ln_g, ln_b):
    T, D = hb.shape
    mem_w = MEM_HEADS * MEM_HD
    q_scale = MEM_HD ** -0.5 * LOG2E
    q = matmul(hb, (w_q * q_scale).astype(BF16), BF16)
    kv = matmul(mem_b.reshape(-1, D), w_kv.astype(BF16), BF16)
    kv3 = kv.reshape(batch, -1, 2 * mem_w)
    o = flash_attention(q.reshape(batch, seq, mem_w), kv3, kv3, n_heads=MEM_HEADS, hq=MEM_HEADS, hk=MEM_HEADS,
                        dk=MEM_HD, dv=MEM_HD, q_blk=0, k_blk=0, v_blk=1, mode="none", kb=kv3.shape[1])
    return matmul_deepnorm_ln(o.reshape(T, mem_w), w_o.astype(BF16), h, ln_g, ln_b)


def peer_ffn(ht, w_q, sub_keys, u_tabs, v_tabs, layer):
    qt = matmul(w_q.T.astype(BF16), ht, F32)
    sk = sub_keys.reshape(2 * PEER_HEADS, N_KEYS, PEER_DKEY // 2).astype(BF16)
    s_t, stats = peer_route(qt, sk)
    u_b, v_b = cast_layer_pair(u_tabs, v_tabs, layer)
    return peer_dense(ht, u_b, v_b, s_t, stats)


def kernel(x, mem, ab_w_in, a_rel_bias, b_q_norm, b_w_uq, b_kv_norm, b_w_ukv, ab_w_out, cd_w_in, d_forget_bias,
           cd_w_out, mem_w_q, mem_w_kv, mem_w_o, peer_w_q, peer_sub_keys, peer_u, peer_v, ln_g, ln_b):
    batch, seq, d_model = x.shape
    h = x.reshape(batch * seq, d_model)
    hb = h.astype(BF16)
    mem_b = mem.astype(BF16)
    for layer in range(DEPTH):
        j = layer // 2
        if layer % 2 == 0:
            y = mixer_ab(hb, batch, seq, ab_w_in[j], a_rel_bias[j], b_q_norm[j], b_w_uq[j], b_kv_norm[j],
                         b_w_ukv[j], ab_w_out[j])
        else:
            y = mixer_cd(hb, batch, seq, cd_w_in[j], d_forget_bias[j], cd_w_out[j])
        h, hb = deepnorm_ln(h, y, ln_g[layer, 0], ln_b[layer, 0])
        h, ht = memory_cross_attention(h, hb, mem_b, batch, seq, mem_w_q[layer], mem_w_kv[layer], mem_w_o[layer],
                                       ln_g[layer, 1], ln_b[layer, 1])
        y = peer_ffn(ht, peer_w_q[layer], peer_sub_keys[layer], peer_u, peer_v, layer)
        h, hb = deepnorm_ln(h, y, ln_g[layer, 2], ln_b[layer, 2])
    return h.reshape(batch, seq, d_model)
```

```python
import functools

import jax
import jax.numpy as jnp
import numpy as np
from jax import lax
from jax.experimental import pallas as pl
from jax.experimental.pallas import tpu as pltpu

F32 = jnp.float32
BF16 = jnp.bfloat16

CHUNK = 64
HEAD_DIM = 128
A_HEADS = 16
LEFT_CHUNKS = 8
REL_CLIP = 256
B_HEADS = 16
MLA_Q_LORA = 1536
MLA_KV_LORA = 512
MLA_NOPE = 128
MLA_ROPE = 64
MLA_V = 128
C_HEADS = 16
C_KV_HEADS = 4
IDX_HEADS = 32
IDX_HD = 64
DSA_TOPK_MAX = 256
D_HEADS = 16
MEM_HEADS = 4
MEM_HD = 128
PEER_HEADS = 8
PEER_DKEY = 256
N_KEYS = 128
PEER_TOPK = 16
ROPE_THETA = 10000.0
LN_EPS = 1e-5
RMS_EPS = 1e-6
DEPTH = 2
DEEPNORM_ALPHA = (2 * DEPTH) ** 0.25

LANES = 128
V7X_VMEM_LIMIT = 56 * 1024 * 1024
MASK_NEG = -1e30
LOG2E = 1.4426950408889634


def _params(sem):
    return pltpu.CompilerParams(dimension_semantics=sem, vmem_limit_bytes=V7X_VMEM_LIMIT)


def _round_up(n, m):
    return (n + m - 1) // m * m


def _mm_kernel(a_ref, w_ref, o_ref):
    o_ref[...] = jnp.dot(a_ref[...], w_ref[...], preferred_element_type=F32).astype(o_ref.dtype)


def matmul(a, w, out_dtype, tm=1024, tn=1024):
    M, K = a.shape
    N = w.shape[1]
    tm = min(tm, M)
    tn = min(tn, N)
    if N % tn:
        tn //= 2
    assert M % tm == 0 and N % tn == 0, (M, N, tm, tn)
    return pl.pallas_call(
        _mm_kernel,
        grid=(M // tm, N // tn),
        in_specs=[pl.BlockSpec((tm, K), lambda i, j: (i, 0)),
                  pl.BlockSpec((K, tn), lambda i, j: (0, j))],
        out_specs=pl.BlockSpec((tm, tn), lambda i, j: (i, j)),
        out_shape=jax.ShapeDtypeStruct((M, N), out_dtype),
        compiler_params=_params(("parallel", "arbitrary")),
        name="matmul",
    )(a, w)


def _mm_rope_kernel(a_ref, w_ref, c_ref, sa_ref, sb_ref, o_ref, *, group, half, rope_blocks):
    def product():
        return jnp.dot(a_ref[...], w_ref[...], preferred_element_type=F32)

    def rotated():
        acc = product()
        c, sa, sb = c_ref[...], sa_ref[...], sb_ref[...]
        for g in range(acc.shape[1] // group):
            x = acc[:, g * group:(g + 1) * group]
            out = x * c + pltpu.roll(x, group - half, 1) * sa + pltpu.roll(x, half, 1) * sb
            o_ref[:, g * group:(g + 1) * group] = out.astype(o_ref.dtype)

    if rope_blocks is None:
        rotated()
    else:
        pl.when(pl.program_id(1) < rope_blocks)(rotated)

        @pl.when(pl.program_id(1) >= rope_blocks)
        def _():
            o_ref[...] = product().astype(o_ref.dtype)


def matmul_rope(a, w, tables, seq, group, half, rope_cols=None, tm=1024, tn=512):
    M, K = a.shape
    N = w.shape[1]
    tm = min(tm, seq)
    assert M % tm == 0 and seq % tm == 0 and N % tn == 0 and tn % group == 0
    assert rope_cols is None or rope_cols % tn == 0
    nsb = seq // tm
    tab = pl.BlockSpec((tm, group), lambda i, j: (i % nsb, 0))
    return pl.pallas_call(
        functools.partial(_mm_rope_kernel, group=group, half=half,
                          rope_blocks=None if rope_cols is None else rope_cols // tn),
        grid=(M // tm, N // tn),
        in_specs=[pl.BlockSpec((tm, K), lambda i, j: (i, 0)),
                  pl.BlockSpec((K, tn), lambda i, j: (0, j)), tab, tab, tab],
        out_specs=pl.BlockSpec((tm, tn), lambda i, j: (i, j)),
        out_shape=jax.ShapeDtypeStruct((M, N), BF16),
        compiler_params=_params(("parallel", "arbitrary")),
        name="matmul_rope",
    )(a, w, *tables)


def _mm2_kernel(a1_ref, a2_ref, w_ref, o_ref):
    a = jnp.concatenate([a1_ref[...], a2_ref[...]], axis=1)
    o_ref[...] = jnp.dot(a, w_ref[...], preferred_element_type=F32).astype(o_ref.dtype)


def matmul_cat(a1, a2, w, out_dtype, tm=1024, tn=512):
    M, K1 = a1.shape
    K2 = a2.shape[1]
    N = w.shape[1]
    tm = min(tm, M)
    tn = min(tn, N)
    assert M % tm == 0 and N % tn == 0 and w.shape[0] == K1 + K2
    return pl.pallas_call(
        _mm2_kernel,
        grid=(M // tm, N // tn),
        in_specs=[pl.BlockSpec((tm, K1), lambda i, j: (i, 0)),
                  pl.BlockSpec((tm, K2), lambda i, j: (i, 0)),
                  pl.BlockSpec((K1 + K2, tn), lambda i, j: (0, j))],
        out_specs=pl.BlockSpec((tm, tn), lambda i, j: (i, j)),
        out_shape=jax.ShapeDtypeStruct((M, N), out_dtype),
        compiler_params=_params(("parallel", "arbitrary")),
        name="matmul_cat",
    )(a1, a2, w)


def _cast_kernel(u_ref, v_ref, ou_ref, ov_ref):
    ou_ref[...] = u_ref[...].astype(ou_ref.dtype)
    ov_ref[...] = v_ref[...].astype(ov_ref.dtype)


def cast_layer_pair(u, v, layer, tr=256):
    _, R, C = u.shape
    src = pl.BlockSpec((None, tr, C), lambda i: (layer, i, 0))
    dst = pl.BlockSpec((tr, C), lambda i: (i, 0))
    return pl.pallas_call(
        _cast_kernel,
        grid=(R // tr,),
        in_specs=[src, src],
        out_specs=[dst, dst],
        out_shape=[jax.ShapeDtypeStruct((R, C), BF16)] * 2,
        compiler_params=_params(("parallel",)),
        name="cast_layer_pair",
    )(u, v)


def _ln_kernel(h_ref, y_ref, g_ref, b_ref, o_ref, ob_ref):
    z = DEEPNORM_ALPHA * h_ref[...] + y_ref[...]
    zc = z - jnp.mean(z, axis=-1, keepdims=True)
    var = jnp.mean(zc * zc, axis=-1, keepdims=True)
    out = zc * lax.rsqrt(var + LN_EPS) * g_ref[...] + b_ref[...]
    o_ref[...] = out
    ob_ref[...] = out.astype(BF16)


def deepnorm_ln(h, y, g, b, tb=256):
    T, D = h.shape
    row = pl.BlockSpec((tb, D), lambda i: (i, 0))
    vec = pl.BlockSpec((1, D), lambda i: (0, 0))
    return pl.pallas_call(
        _ln_kernel,
        grid=(T // tb,),
        in_specs=[row, row, vec, vec],
        out_specs=[row, row],
        out_shape=[jax.ShapeDtypeStruct((T, D), F32), jax.ShapeDtypeStruct((T, D), BF16)],
        compiler_params=_params(("parallel",)),
        name="deepnorm_ln",
    )(h, y, g.reshape(1, D), b.reshape(1, D))


def _mm_ln_kernel(a_ref, w_ref, h_ref, g_ref, b_ref, o_ref, obt_ref):
    z = DEEPNORM_ALPHA * h_ref[...] + jnp.dot(a_ref[...], w_ref[...], preferred_element_type=F32)
    zc = z - jnp.mean(z, axis=-1, keepdims=True)
    var = jnp.mean(zc * zc, axis=-1, keepdims=True)
    out = zc * lax.rsqrt(var + LN_EPS) * g_ref[...] + b_ref[...]
    o_ref[...] = out
    obt_ref[...] = out.T.astype(BF16)


def matmul_deepnorm_ln(a, w, h, g, b, tb=256):
    T, D = h.shape
    K = a.shape[1]
    row = pl.BlockSpec((tb, D), lambda i: (i, 0))
    vec = pl.BlockSpec((1, D), lambda i: (0, 0))
    return pl.pallas_call(
        _mm_ln_kernel,
        grid=(T // tb,),
        in_specs=[pl.BlockSpec((tb, K), lambda i: (i, 0)), pl.BlockSpec((K, D), lambda i: (0, 0)), row, vec, vec],
        out_specs=[row, pl.BlockSpec((D, tb), lambda i: (0, i))],
        out_shape=[jax.ShapeDtypeStruct((T, D), F32), jax.ShapeDtypeStruct((D, T), BF16)],
        compiler_params=_params(("parallel",)),
        name="matmul_deepnorm_ln",
    )(a, w, h, g.reshape(1, D), b.reshape(1, D))


def _rms_kernel(x_ref, g_ref, o_ref):
    x = x_ref[...].astype(F32)
    o_ref[...] = (x * lax.rsqrt(jnp.mean(x * x, axis=-1, keepdims=True) + RMS_EPS) * g_ref[...]).astype(o_ref.dtype)


def rms_norm_cols(x, col_block, width, g, tb=512):
    T = x.shape[0]
    return pl.pallas_call(
        _rms_kernel,
        grid=(T // tb,),
        in_specs=[pl.BlockSpec((tb, width), lambda i: (i, col_block)),
                  pl.BlockSpec((1, width), lambda i: (0, 0))],
        out_specs=pl.BlockSpec((tb, width), lambda i: (i, 0)),
        out_shape=jax.ShapeDtypeStruct((T, width), BF16),
        compiler_params=_params(("parallel",)),
        name="rms_norm",
    )(x, g.reshape(1, width))


def rope_tables(seq, dim, group, offset=0, reps=1, scale=1.0):
    half = dim // 2
    inv = ROPE_THETA ** (-jnp.arange(half, dtype=F32) * 2.0 / dim)
    ang = jnp.arange(seq, dtype=F32)[:, None] * inv[None, :]
    cos, sin = jnp.cos(ang), jnp.sin(ang)
    zeros_tail = jnp.zeros((seq, group - offset - dim * reps), F32)
    head = jnp.ones((seq, offset), F32)
    zhead = jnp.zeros((seq, offset), F32)
    zhalf = jnp.zeros((seq, half), F32)
    c = jnp.concatenate([head] + [cos, cos] * reps + [zeros_tail], axis=1)
    sa = jnp.concatenate([zhead] + [-sin, zhalf] * reps + [zeros_tail], axis=1)
    sb = jnp.concatenate([zhead] + [zhalf, sin] * reps + [zeros_tail], axis=1)
    return c * scale, sa * scale, sb * scale


def _rope_kernel(x_ref, c_ref, sa_ref, sb_ref, o_ref, *, groups, group, half):
    c, sa, sb = c_ref[...], sa_ref[...], sb_ref[...]
    for g in range(groups):
        x = x_ref[:, g * group:(g + 1) * group].astype(F32)
        out = x * c + pltpu.roll(x, group - half, 1) * sa + pltpu.roll(x, half, 1) * sb
        o_ref[:, g * group:(g + 1) * group] = out.astype(o_ref.dtype)


def rope_cols(x, col_block, groups, group, half, tables, seq, tb=256):
    T = x.shape[0]
    W = groups * group
    nsb = seq // tb
    tab = pl.BlockSpec((tb, group), lambda i: (i % nsb, 0))
    return pl.pallas_call(
        functools.partial(_rope_kernel, groups=groups, group=group, half=half),
        grid=(T // tb,),
        in_specs=[pl.BlockSpec((tb, W), lambda i: (i, col_block)), tab, tab, tab],
        out_specs=pl.BlockSpec((tb, W), lambda i: (i, 0)),
        out_shape=jax.ShapeDtypeStruct((T, W), BF16),
        compiler_params=_params(("parallel",)),
        name="rope",
    )(x, *tables)


def _mla_kv_kernel(kv_ref, kr_ref, c_ref, sa_ref, sb_ref, k_ref, v_ref, *, heads, half):
    kr = kr_ref[...].astype(F32)
    kr = kr * c_ref[...] + pltpu.roll(kr, LANES - half, 1) * sa_ref[...] + pltpu.roll(kr, half, 1) * sb_ref[...]
    kr = kr.astype(k_ref.dtype)
    for h in range(heads):
        k_ref[:, 2 * h * LANES:(2 * h + 1) * LANES] = kv_ref[:, 2 * h * LANES:(2 * h + 1) * LANES]
        k_ref[:, (2 * h + 1) * LANES:(2 * h + 2) * LANES] = kr
        v_ref[:, h * LANES:(h + 1) * LANES] = kv_ref[:, (2 * h + 1) * LANES:(2 * h + 2) * LANES]


def mla_assemble_kv(kv, cproj, kr_col_block, tables, seq, tb=256):
    T, W = kv.shape
    nsb = seq // tb
    tab = pl.BlockSpec((tb, LANES), lambda i: (i % nsb, 0))
    return pl.pallas_call(
        functools.partial(_mla_kv_kernel, heads=B_HEADS, half=MLA_ROPE // 2),
        grid=(T // tb,),
        in_specs=[pl.BlockSpec((tb, W), lambda i: (i, 0)),
                  pl.BlockSpec((tb, LANES), lambda i: (i, kr_col_block)), tab, tab, tab],
        out_specs=[pl.BlockSpec((tb, W), lambda i: (i, 0)), pl.BlockSpec((tb, W // 2), lambda i: (i, 0))],
        out_shape=[jax.ShapeDtypeStruct((T, W), BF16), jax.ShapeDtypeStruct((T, W // 2), BF16)],
        compiler_params=_params(("parallel",)),
        name="mla_assemble_kv",
    )(kv, cproj, *tables)


FLASH_RB = 128


def _flash_kernel(*refs, hq, hk, dk, dv, qb, kb, mode, fox, q_axis, n_kv_blocks):
    it = iter(refs)
    q_ref, k_ref, v_ref = next(it), next(it), next(it)
    ct_ref = cs_ref = bias_ref = None
    if fox:
        ct_ref, cs_ref = next(it), next(it)
    if mode == "bias":
        bias_ref = next(it)
    o_ref, m_ref, l_ref, acc_ref, p_ref, alpha_ref = next(it), next(it), next(it), next(it), next(it), next(it)
    ct_lanes_ref = next(it) if fox else None

    qi = pl.program_id(q_axis)
    m_ref[...] = jnp.full(m_ref.shape, MASK_NEG, F32)
    l_ref[...] = jnp.zeros(l_ref.shape, F32)
    acc_ref[...] = jnp.zeros(acc_ref.shape, F32)
    if fox:
        for h in range(hq):
            ct_lanes_ref[h] = jnp.broadcast_to(ct_ref[0, h] * LOG2E, (qb, LANES))

    n_sub = qb // FLASH_RB
    per_kv = kb // FLASH_RB
    rows = lax.broadcasted_iota(jnp.int32, (FLASH_RB, kb), 0)
    cols = lax.broadcasted_iota(jnp.int32, (FLASH_RB, kb), 1)

    def diag_mask(r):
        row_in_kv = rows + (r % per_kv) * FLASH_RB
        if mode == "causal":
            return cols <= row_in_kv
        if mode == "chunk":
            return (cols // CHUNK) <= (row_in_kv // CHUNK)
        return None

    def step(j, visible):
        ks = pl.multiple_of(j * kb, kb)
        for h in range(hq):
            g = h if hk == hq else 0
            k = k_ref[0, pl.ds(ks, kb), g * dk:(g + 1) * dk]
            v = v_ref[0, pl.ds(ks, kb), g * dv:(g + 1) * dv]
            s_all = lax.dot_general(q_ref[0, :, h * dk:(h + 1) * dk], k, (((1,), (1,)), ((), ())),
                                    preferred_element_type=F32)
            cs = cs_ref[0, h, :, pl.ds(ks, kb)] * LOG2E if fox else None
            for r in range(n_sub):
                rs = slice(r * FLASH_RB, (r + 1) * FLASH_RB)
                if visible[r] == "none":
                    p_ref[h, rs, :] = jnp.zeros((FLASH_RB, kb), p_ref.dtype)
                    alpha_ref[h, rs, :] = jnp.ones((FLASH_RB, LANES), F32)
                    continue
                s = s_all[rs]
                if bias_ref is not None:
                    s = s + bias_ref[0, rs, pl.ds(ks, kb)].astype(F32)
                blocks = [s[:, c * LANES:(c + 1) * LANES] for c in range(kb // LANES)]
                if fox:
                    ct = ct_lanes_ref[h, rs]
                    blocks = [blk + ct - cs[:, c * LANES:(c + 1) * LANES] for c, blk in enumerate(blocks)]
                if visible[r] == "diag" and mode in ("causal", "chunk"):
                    mask = diag_mask(r)
                    blocks = [jnp.where(mask[:, c * LANES:(c + 1) * LANES], blk, MASK_NEG)
                              for c, blk in enumerate(blocks)]
                blk_max = blocks[0]
                for blk in blocks[1:]:
                    blk_max = jnp.maximum(blk_max, blk)
                m_prev = m_ref[h, rs]
                m_new = jnp.maximum(m_prev, jnp.max(blk_max, axis=1, keepdims=True))
                alpha = jnp.exp2(m_prev - m_new)
                p_blocks = [jnp.exp2(blk - m_new) for blk in blocks]
                p_lanes = p_blocks[0]
                for pb in p_blocks[1:]:
                    p_lanes = p_lanes + pb
                l_ref[h, rs] = alpha * l_ref[h, rs] + p_lanes
                for c, pb in enumerate(p_blocks):
                    p_ref[h, rs, c * LANES:(c + 1) * LANES] = pb.astype(p_ref.dtype)
                alpha_ref[h, rs, :] = alpha
                m_ref[h, rs] = m_new
            pv = jnp.dot(p_ref[h], v, preferred_element_type=F32)
            acc_ref[h] = alpha_ref[h] * acc_ref[h] + pv

    def body(j, carry):
        step(j, ["full"] * n_sub)
        return carry

    if mode == "none":
        lax.fori_loop(0, n_kv_blocks, body, 0)
    else:
        kv_per_q = qb // kb
        lax.fori_loop(0, qi * kv_per_q, body, 0)
        for t in range(kv_per_q):
            kinds = ["full" if r // per_kv > t else "diag" if r // per_kv == t else "none" for r in range(n_sub)]
            step(qi * kv_per_q + t, kinds)

    for h in range(hq):
        l = jnp.sum(l_ref[h], axis=1, keepdims=True)
        o_ref[0, :, h * dv:(h + 1) * dv] = (acc_ref[h] / l).astype(o_ref.dtype)


def flash_attention(q, k, v, *, n_heads, hq, hk, dk, dv, q_blk, k_blk, v_blk, mode,
                    qb=512, kb=512, fox=None, bias=None):
    B, Sq = q.shape[0], q.shape[1]
    Skv = k.shape[1]
    kb = min(kb, Skv)
    nq, ng = Sq // qb, n_heads // hq
    assert hk in (hq, 1) and qb % kb == 0 and kb % FLASH_RB == 0
    if mode in ("causal", "chunk", "bias"):
        assert Sq == Skv
    if mode != "bias":
        grid = (B, ng, nq)
        q_axis = 2

        def spec(shape, fn):
            return pl.BlockSpec(shape, lambda b, g, i: fn(b, g, i))
    else:
        grid = (B, nq, ng)
        q_axis = 1

        def spec(shape, fn):
            return pl.BlockSpec(shape, lambda b, i, g: fn(b, g, i))

    in_specs = [
        spec((1, qb, hq * dk), lambda b, g, i: (b, i, q_blk + g)),
        spec((1, Skv, hk * dk), lambda b, g, i: (b, 0, k_blk + g)),
        spec((1, Skv, hk * dv), lambda b, g, i: (b, 0, v_blk + g)),
    ]
    args = [q, k, v]
    if fox is not None:
        ct, cs = fox
        in_specs += [spec((1, hq, qb, 1), lambda b, g, i: (b, g, i, 0)),
                     spec((1, hq, 1, Skv), lambda b, g, i: (b, g, 0, 0))]
        args += [ct, cs]
    if mode == "bias":
        in_specs.append(spec((1, qb, Skv), lambda b, g, i: (b, i, 0)))
        args.append(bias)
    assert dv == LANES
    stat = pltpu.VMEM((hq, qb, LANES), F32)
    kern = functools.partial(_flash_kernel, hq=hq, hk=hk, dk=dk, dv=dv, qb=qb, kb=kb, mode=mode,
                             fox=fox is not None, q_axis=q_axis, n_kv_blocks=Skv // kb)
    return pl.pallas_call(
        kern,
        grid=grid,
        in_specs=in_specs,
        out_specs=spec((1, qb, hq * dv), lambda b, g, i: (b, i, g)),
        out_shape=jax.ShapeDtypeStruct((B, Sq, n_heads * dv), BF16),
        scratch_shapes=[stat, stat, pltpu.VMEM((hq, qb, dv), F32), pltpu.VMEM((hq, qb, kb), BF16), stat]
        + ([stat] if fox is not None else []),
        compiler_params=_params(("parallel", "parallel", "arbitrary")),
        name="flash_" + mode,
    )(*args)


BAND_QB = 4 * CHUNK
BAND_KBLOCKS = (LEFT_CHUNKS * CHUNK) // BAND_QB + 1
BAND_W = BAND_KBLOCKS * BAND_QB


def band_bias_tiles(rel_bias):
    pad = LEFT_CHUNKS * CHUNK
    period = BAND_QB + BAND_W
    m = np.arange(period)
    j_minus_i = np.where(m < period - BAND_QB, m, m - period)
    r = rel_bias[:, np.clip(pad - j_minus_i, -REL_CLIP, REL_CLIP) + REL_CLIP].astype(F32)
    skew = jnp.tile(r, (1, BAND_QB))[:, :BAND_QB * (period - 1)].reshape(-1, BAND_QB, period - 1)
    i = np.arange(BAND_QB)[:, None]
    j = np.arange(BAND_W)[None, :]
    chunk_diff = i // CHUNK + LEFT_CHUNKS - j // CHUNK
    in_band = (chunk_diff >= 0) & (chunk_diff <= LEFT_CHUNKS)
    return jnp.where(in_band[None], skew[:, :, :BAND_W] * LOG2E, MASK_NEG)


def _band_kernel(q_ref, k_ref, v_ref, bias_ref, o_ref, s_ref, p_ref, l_ref, *, hb):
    qi = pl.program_id(2)
    for h in range(hb):
        lanes = slice(h * HEAD_DIM, (h + 1) * HEAD_DIM)
        q = q_ref[0, :, lanes]
        v_parts = []
        for jb in range(BAND_KBLOCKS):
            kblk = qi - (BAND_KBLOCKS - 1) + jb
            ks = pl.multiple_of(jnp.maximum(kblk, 0) * BAND_QB, BAND_QB)
            k = k_ref[0, pl.ds(ks, BAND_QB), lanes]
            v_parts.append(v_ref[0, pl.ds(ks, BAND_QB), lanes])
            s = lax.dot_general(q, k, (((1,), (1,)), ((), ())), preferred_element_type=F32)
            s = s + bias_ref[h, :, jb * BAND_QB:(jb + 1) * BAND_QB]
            s_ref[h, :, jb * BAND_QB:(jb + 1) * BAND_QB] = jnp.where(kblk >= 0, s, MASK_NEG)
        for r in range(BAND_QB // CHUNK):
            rs = slice(r * CHUNK, (r + 1) * CHUNK)
            blocks = [s_ref[h, rs, c * LANES:(c + 1) * LANES] for c in range(BAND_W // LANES)]
            blk_max = blocks[0]
            for blk in blocks[1:]:
                blk_max = jnp.maximum(blk_max, blk)
            m = jnp.broadcast_to(jnp.max(blk_max, axis=1, keepdims=True), (CHUNK, LANES))
            p_lanes = jnp.zeros((CHUNK, LANES), F32)
            for c, blk in enumerate(blocks):
                p = jnp.exp2(blk - m)
                p_lanes = p_lanes + p
                p_ref[h, rs, c * LANES:(c + 1) * LANES] = p.astype(p_ref.dtype)
            l_ref[h, rs, :] = p_lanes
        acc = jnp.zeros((BAND_QB, HEAD_DIM), F32)
        for jb in range(BAND_KBLOCKS):
            acc = acc + jnp.dot(p_ref[h, :, jb * BAND_QB:(jb + 1) * BAND_QB], v_parts[jb], preferred_element_type=F32)
        l = jnp.sum(l_ref[h], axis=1, keepdims=True)
        o_ref[0, :, lanes] = (acc / l).astype(o_ref.dtype)


def band_attention(qkv, bias_tiles, hb=4):
    B, S, _ = qkv.shape
    H = A_HEADS
    ng = H // hb
    return pl.pallas_call(
        functools.partial(_band_kernel, hb=hb),
        grid=(B, ng, S // BAND_QB),
        in_specs=[pl.BlockSpec((1, BAND_QB, hb * HEAD_DIM), lambda b, g, i: (b, i, g)),
                  pl.BlockSpec((1, S, hb * HEAD_DIM), lambda b, g, i: (b, 0, ng + g)),
                  pl.BlockSpec((1, S, hb * HEAD_DIM), lambda b, g, i: (b, 0, 2 * ng + g)),
                  pl.BlockSpec((hb, BAND_QB, BAND_W), lambda b, g, i: (g, 0, 0))],
        out_specs=pl.BlockSpec((1, BAND_QB, hb * HEAD_DIM), lambda b, g, i: (b, i, g)),
        out_shape=jax.ShapeDtypeStruct((B, S, H * HEAD_DIM), BF16),
        scratch_shapes=[pltpu.VMEM((hb, BAND_QB, BAND_W), F32), pltpu.VMEM((hb, BAND_QB, BAND_W), BF16),
                        pltpu.VMEM((hb, BAND_QB, LANES), F32)],
        compiler_params=_params(("parallel", "parallel", "arbitrary")),
        name="band_attention",
    )(qkv, qkv, qkv, bias_tiles)


def _fox_cumsum_kernel(f_ref, b_ref, o_ref, carry_ref, *, cb):
    @pl.when(pl.program_id(1) == 0)
    def _():
        carry_ref[...] = jnp.zeros(carry_ref.shape, F32)

    x = f_ref[0].astype(F32) + b_ref[...]
    log_f = jnp.minimum(x, 0.0) - jnp.log1p(jnp.exp(-jnp.abs(x)))
    r = lax.broadcasted_iota(jnp.int32, (cb, cb), 0)
    c = lax.broadcasted_iota(jnp.int32, (cb, cb), 1)
    tri = jnp.where(c <= r, 1.0, 0.0).astype(F32)
    cum = jnp.dot(tri, log_f, preferred_element_type=F32, precision=lax.Precision.HIGHEST) + carry_ref[...]
    o_ref[0] = cum
    carry_ref[...] = cum[cb - 1:cb, :]


def fox_cumsum(proj, col_block, bias_row, batch, seq, cb=256):
    x = proj.reshape(batch, seq, proj.shape[-1])
    return pl.pallas_call(
        functools.partial(_fox_cumsum_kernel, cb=cb),
        grid=(batch, seq // cb),
        in_specs=[pl.BlockSpec((1, cb, LANES), lambda b, i: (b, i, col_block)),
                  pl.BlockSpec((1, LANES), lambda b, i: (0, 0))],
        out_specs=pl.BlockSpec((1, cb, LANES), lambda b, i: (b, i, 0)),
        out_shape=jax.ShapeDtypeStruct((batch, seq, LANES), F32),
        scratch_shapes=[pltpu.VMEM((1, LANES), F32)],
        compiler_params=_params(("parallel", "arbitrary")),
        name="fox_cumsum",
    )(x, bias_row)


IDX_QB = 128
IDX_KC = 512


def _sortable_key(x):
    bits = pltpu.bitcast(x, jnp.int32)
    return bits ^ ((bits >> 31) & jnp.int32(0x7FFFFFFF))


def _indexer_kernel(ki_ref, qit_ref, w_ref, o_ref, key_ref, *, seq, topk):
    qi = pl.program_id(1)
    q0 = qi * IDX_QB
    n_chunks = (q0 + IDX_QB + IDX_KC - 1) // IDX_KC
    qchunk = (q0 + lax.broadcasted_iota(jnp.int32, (1, IDX_QB), 1)) // CHUNK
    int_min = jnp.int32(-2 ** 31)
    w_scale = IDX_HEADS ** -0.5 * IDX_HD ** -0.5

    def score_chunk(c, carry):
        ks = pl.multiple_of(c * IDX_KC, IDX_KC)
        ki = ki_ref[0, pl.ds(ks, IDX_KC), :][:, :IDX_HD]
        acc = jnp.zeros((IDX_KC, IDX_QB), F32)
        for hp in range(IDX_HEADS // 2):
            t = jnp.dot(ki, qit_ref[0, 0, :, hp * 2 * IDX_QB:(hp + 1) * 2 * IDX_QB], preferred_element_type=F32)
            w = w_ref[0, 0, :, hp * 2 * IDX_QB:(hp + 1) * 2 * IDX_QB] * w_scale
            t = jnp.maximum(t, 0.0) * w
            acc = acc + t[:, :IDX_QB] + t[:, IDX_QB:]
        kchunk = (ks + lax.broadcasted_iota(jnp.int32, (IDX_KC, 1), 0)) // CHUNK
        acc = jnp.where(kchunk <= qchunk, acc, -jnp.inf)
        key_ref[pl.ds(ks, IDX_KC), :] = _sortable_key(acc)
        return carry

    lax.fori_loop(0, n_chunks, score_chunk, 0)

    def count_ge(cand):
        def body(c, acc):
            ks = pl.multiple_of(c * IDX_KC, IDX_KC)
            blk = key_ref[pl.ds(ks, IDX_KC), :]
            hit = jnp.where(blk >= cand, 1, 0).astype(jnp.int32)
            return acc + jnp.sum(hit.reshape(IDX_KC // 8, 8, IDX_QB), axis=0)
        acc = lax.fori_loop(0, n_chunks, body, jnp.zeros((8, IDX_QB), jnp.int32))
        return jnp.sum(acc, axis=0, keepdims=True)

    def bit_step(i, ans):
        bit = lax.shift_left(jnp.int32(1), 31 - i)
        cand = ans | bit
        cnt = count_ge(cand ^ int_min)
        return jnp.where(cnt >= topk, cand, ans)

    ans = lax.fori_loop(0, 32, bit_step, jnp.zeros((1, IDX_QB), jnp.int32))
    thr = ans ^ int_min
    neg_inf_key = _sortable_key(jnp.full((1, IDX_QB), -jnp.inf, F32))

    def emit(c, carry):
        ks = pl.multiple_of(c * IDX_KC, IDX_KC)
        blk = key_ref[pl.ds(ks, IDX_KC), :]
        sel = (blk >= thr) & (blk > neg_inf_key)
        bias_t = jnp.where(sel, 0.0, MASK_NEG).astype(F32)
        o_ref[0, :, pl.ds(ks, IDX_KC)] = bias_t.T.astype(o_ref.dtype)
        return carry

    lax.fori_loop(0, n_chunks, emit, 0)

    def fill(c, carry):
        ks = pl.multiple_of(c * IDX_KC, IDX_KC)
        o_ref[0, :, pl.ds(ks, IDX_KC)] = jnp.full((IDX_QB, IDX_KC), MASK_NEG, o_ref.dtype)
        return carry

    lax.fori_loop(n_chunks, seq // IDX_KC, fill, 0)


def dsa_selection_bias(ki, qit, wt, topk):
    B, S, _ = ki.shape
    nq = S // IDX_QB
    return pl.pallas_call(
        functools.partial(_indexer_kernel, seq=S, topk=topk),
        grid=(B, nq),
        in_specs=[pl.BlockSpec((1, S, LANES), lambda b, i: (b, 0, 0)),
                  pl.BlockSpec((1, 1, IDX_HD, IDX_HEADS * IDX_QB), lambda b, i: (b, i, 0, 0)),
                  pl.BlockSpec((1, 1, 1, IDX_HEADS * IDX_QB), lambda b, i: (b, i, 0, 0))],
        out_specs=pl.BlockSpec((1, IDX_QB, S), lambda b, i: (b, i, 0)),
        out_shape=jax.ShapeDtypeStruct((B, S, S), BF16),
        scratch_shapes=[pltpu.VMEM((S, IDX_QB), jnp.int32)],
        compiler_params=_params(("parallel", "arbitrary")),
        name="dsa_indexer",
    )(ki, qit, wt)


PEER_TB_ROUTE = 128
STAT_ROWS = 8


def _peer_route_kernel(qt_ref, sk_ref, s_ref, st_ref, top_ref):
    half = PEER_DKEY // 2
    tb = PEER_TB_ROUTE
    for hp in range(2 * PEER_HEADS):
        q = qt_ref[hp * half:(hp + 1) * half, :].astype(BF16)
        s_ref[hp] = jnp.dot(sk_ref[hp], q, preferred_element_type=F32)

    def head_tops(h, carry):
        def extract_pair(i, c):
            xa, xb = c
            ma = jnp.max(xa, axis=0, keepdims=True)
            mb = jnp.max(xb, axis=0, keepdims=True)
            top_ref[2 * h, pl.ds(i, 1), :] = ma
            top_ref[2 * h + 1, pl.ds(i, 1), :] = mb
            return jnp.where(xa == ma, -jnp.inf, xa), jnp.where(xb == mb, -jnp.inf, xb)

        lax.fori_loop(0, PEER_TOPK, extract_pair, (s_ref[2 * h], s_ref[2 * h + 1]))
        return carry

    lax.fori_loop(0, PEER_HEADS, head_tops, 0)

    def candidates(h):
        a = top_ref[2 * h]
        b = top_ref[2 * h + 1]
        return jnp.concatenate([a[0:1, :] + b] + [a[i:i + 1, :] + b[0:8, :] for i in range(1, 8)]
                               + [a[8:PEER_TOPK, :] + b[0:1, :]], axis=0)

    def extract(i, x, mx, z):
        m = jnp.max(x, axis=0, keepdims=True)
        mx = jnp.where(i == 0, m, mx)
        return jnp.where(x == m, -jnp.inf, x), mx, z + jnp.exp(m - mx), m

    def pair_stats(hh, carry):
        h0, h1 = 2 * hh, 2 * hh + 1

        def extract2(i, c):
            x0, mx0, z0, _, x1, mx1, z1, _ = c
            return extract(i, x0, mx0, z0) + extract(i, x1, mx1, z1)

        zero = jnp.zeros((1, tb), F32)
        res = lax.fori_loop(0, PEER_TOPK, extract2, (candidates(h0), zero, zero, zero, candidates(h1), zero, zero, zero))
        for h, (_, _, z, last) in ((h0, res[:4]), (h1, res[4:])):
            st_ref[h, 0:1, :] = last
            st_ref[h, 1:2, :] = top_ref[2 * h, 0:1, :]
            st_ref[h, 2:3, :] = top_ref[2 * h + 1, 0:1, :]
            st_ref[h, 3:4, :] = 1.0 / z
            st_ref[h, 4:STAT_ROWS, :] = jnp.zeros((STAT_ROWS - 4, tb), F32)
        return carry

    lax.fori_loop(0, PEER_HEADS // 2, pair_stats, 0)


def peer_route(qt, sub_keys):
    R, T = qt.shape
    tb = PEER_TB_ROUTE
    nsub = 2 * PEER_HEADS
    return pl.pallas_call(
        _peer_route_kernel,
        grid=(T // tb,),
        in_specs=[pl.BlockSpec((R, tb), lambda i: (0, i)),
                  pl.BlockSpec((nsub, N_KEYS, PEER_DKEY // 2), lambda i: (0, 0, 0))],
        out_specs=[pl.BlockSpec((nsub, N_KEYS, tb), lambda i: (0, 0, i)),
                   pl.BlockSpec((PEER_HEADS, STAT_ROWS, tb), lambda i: (0, 0, i))],
        out_shape=[jax.ShapeDtypeStruct((nsub, N_KEYS, T), F32),
                   jax.ShapeDtypeStruct((PEER_HEADS, STAT_ROWS, T), F32)],
        scratch_shapes=[pltpu.VMEM((nsub, PEER_TOPK, tb), F32)],
        compiler_params=_params(("parallel",)),
        name="peer_route",
    )(qt, sub_keys)


PEER_TB = 512
PEER_EB = 512


def _gelu_exact(x):
    return 0.5 * x * (1.0 + lax.erf(x * (2.0 ** -0.5)))


def _peer_dense_kernel(xt_ref, u_ref, v_ref, s_ref, st_ref, y_ref, e2_ref, ht_ref):
    e = pl.program_id(1)

    @pl.when(e == 0)
    def _():
        y_ref[...] = jnp.zeros(y_ref.shape, F32)
        for h in range(PEER_HEADS):
            e2_ref[h] = jnp.exp(s_ref[2 * h + 1] - st_ref[h, 2:3, :])

    act = _gelu_exact(jnp.dot(u_ref[...], xt_ref[...], preferred_element_type=F32))
    rows_per_step = PEER_EB // N_KEYS
    for r in range(rows_per_step):
        i1 = e * rows_per_step + r
        gate = jnp.zeros((N_KEYS, PEER_TB), F32)
        for h in range(PEER_HEADS):
            s1 = s_ref[2 * h, pl.ds(i1, 1), :]
            e1 = jnp.exp(s1 - st_ref[h, 1:2, :]) * st_ref[h, 3:4, :]
            pair = s_ref[2 * h + 1] + s1
            gate = gate + jnp.where(pair >= st_ref[h, 0:1, :], e2_ref[h] * e1, 0.0)
        ht_ref[r * N_KEYS:(r + 1) * N_KEYS, :] = (gate * act[r * N_KEYS:(r + 1) * N_KEYS, :]).astype(BF16)
    y_ref[...] += lax.dot_general(ht_ref[...], v_ref[...], (((0,), (0,)), ((), ())), preferred_element_type=F32)


def peer_dense(xt, u, v, s_t, stats):
    D, T = xt.shape
    E = u.shape[0]
    nsub = 2 * PEER_HEADS
    once = pl.Buffered(1)
    return pl.pallas_call(
        _peer_dense_kernel,
        grid=(T // PEER_TB, E // PEER_EB),
        in_specs=[pl.BlockSpec((D, PEER_TB), lambda i, e: (0, i), pipeline_mode=once),
                  pl.BlockSpec((PEER_EB, D), lambda i, e: (e, 0)),
                  pl.BlockSpec((PEER_EB, D), lambda i, e: (e, 0)),
                  pl.BlockSpec((nsub, N_KEYS, PEER_TB), lambda i, e: (0, 0, i), pipeline_mode=once),
                  pl.BlockSpec((PEER_HEADS, STAT_ROWS, PEER_TB), lambda i, e: (0, 0, i))],
        out_specs=pl.BlockSpec((PEER_TB, D), lambda i, e: (i, 0)),
        out_shape=jax.ShapeDtypeStruct((T, D), F32),
        scratch_shapes=[pltpu.VMEM((PEER_HEADS, N_KEYS, PEER_TB), F32), pltpu.VMEM((PEER_EB, PEER_TB), BF16)],
        compiler_params=_params(("parallel", "arbitrary")),
        name="peer_dense",
    )(xt, u, v, s_t, stats)


def _pad_cols(w, width):
    return jnp.pad(w, ((0, 0), (0, width - w.shape[1])))


def mixer_ab(hb, batch, seq, w_in, rel_bias, q_norm, w_uq, kv_norm, w_ukv, w_out):
    T, D = hb.shape
    a_w = A_HEADS * HEAD_DIM
    qa_scale = HEAD_DIM ** -0.5 * LOG2E
    w_qkv = jnp.concatenate([w_in[:, :a_w] * qa_scale, w_in[:, a_w:3 * a_w]], axis=1).astype(BF16)
    c_cols = MLA_Q_LORA + MLA_KV_LORA + LANES
    w_c = _pad_cols(w_in[:, 3 * a_w:], _round_up(c_cols, 512)).astype(BF16)
    qkv = matmul(hb, w_qkv, BF16)
    cproj = matmul(hb, w_c, F32)

    oa = band_attention(qkv.reshape(batch, seq, 3 * a_w), band_bias_tiles(rel_bias))

    qh = MLA_NOPE + MLA_ROPE
    w_uq_p = jnp.pad(w_uq.reshape(MLA_Q_LORA, B_HEADS, qh), ((0, 0), (0, 0), (0, 2 * LANES - qh)))
    w_uq_p = w_uq_p.reshape(MLA_Q_LORA, B_HEADS * 2 * LANES).astype(BF16)
    cq_n = rms_norm_cols(cproj, 0, MLA_Q_LORA, q_norm)
    ckv_n = rms_norm_cols(cproj, MLA_Q_LORA // MLA_KV_LORA, MLA_KV_LORA, kv_norm)
    q_scale = (MLA_NOPE + MLA_ROPE) ** -0.5 * LOG2E
    q_cat = matmul_rope(cq_n, w_uq_p, rope_tables(seq, MLA_ROPE, 2 * LANES, offset=MLA_NOPE, scale=q_scale),
                        seq, 2 * LANES, MLA_ROPE // 2)
    kv = matmul(ckv_n, w_ukv.astype(BF16), BF16)
    k_cat, v_cat = mla_assemble_kv(kv, cproj, (MLA_Q_LORA + MLA_KV_LORA) // LANES,
                                   rope_tables(seq, MLA_ROPE, LANES), seq)
    ob = flash_attention(q_cat.reshape(batch, seq, -1), k_cat.reshape(batch, seq, -1), v_cat.reshape(batch, seq, -1),
                         n_heads=B_HEADS, hq=4, hk=4, dk=2 * LANES, dv=MLA_V, q_blk=0, k_blk=0, v_blk=0,
                         mode="chunk")
    return matmul_cat(oa.reshape(T, -1), ob.reshape(T, -1), w_out.astype(BF16), F32)


def mixer_cd(hb, batch, seq, w_in, forget_bias, w_out):
    T, D = hb.shape
    c_qw, c_kw, d_w = C_HEADS * HEAD_DIM, C_KV_HEADS * HEAD_DIM, D_HEADS * HEAD_DIM
    i_w = IDX_HEADS * IDX_HD
    offs = np.cumsum([0, c_qw, c_kw, c_kw, i_w, IDX_HD, IDX_HEADS, d_w, d_w, d_w, D_HEADS])
    col = lambda n: w_in[:, offs[n]:offs[n + 1]]
    q_scale = HEAD_DIM ** -0.5 * LOG2E
    w_a = jnp.concatenate([col(0) * q_scale, col(1), col(2), col(6) * q_scale, col(7), col(8)], axis=1).astype(BF16)
    w_b = jnp.concatenate([col(3), _pad_cols(col(4), LANES), _pad_cols(col(5), LANES), _pad_cols(col(9), LANES)], axis=1)
    w_b = _pad_cols(w_b, _round_up(w_b.shape[1], 512)).astype(BF16)
    qk_w = c_qw + c_kw
    proj_a = matmul_rope(hb, w_a, rope_tables(seq, HEAD_DIM, LANES), seq, LANES, HEAD_DIM // 2, rope_cols=qk_w)
    proj_b = matmul(hb, w_b, F32)
    blk_ki, blk_wi, blk_fd = i_w // LANES, i_w // LANES + 1, i_w // LANES + 2

    qi_rot = rope_cols(proj_b, 0, i_w // LANES, LANES, IDX_HD // 2, rope_tables(seq, IDX_HD, LANES, reps=2), seq)
    ki_rot = rope_cols(proj_b, blk_ki, 1, LANES, IDX_HD // 2, rope_tables(seq, IDX_HD, LANES), seq)
    nq = seq // IDX_QB
    qit = qi_rot.reshape(batch, nq, IDX_QB, IDX_HEADS, IDX_HD).transpose(0, 1, 4, 3, 2)
    qit = qit.reshape(batch, nq, IDX_HD, IDX_HEADS * IDX_QB)
    wi = proj_b[:, blk_wi * LANES:blk_wi * LANES + IDX_HEADS]
    wt = wi.reshape(batch, nq, IDX_QB, IDX_HEADS).transpose(0, 1, 3, 2).reshape(batch, nq, 1, IDX_HEADS * IDX_QB)
    sel_bias = dsa_selection_bias(ki_rot.reshape(batch, seq, LANES), qit, wt, min(DSA_TOPK_MAX, seq // 4))
    pa3 = proj_a.reshape(batch, seq, -1)
    rep = C_HEADS // C_KV_HEADS
    oc = flash_attention(pa3, pa3, pa3, n_heads=C_HEADS, hq=rep, hk=1, dk=HEAD_DIM, dv=HEAD_DIM,
                         q_blk=0, k_blk=c_qw // HEAD_DIM, v_blk=qk_w // HEAD_DIM, mode="bias", bias=sel_bias)

    fbias = _pad_cols(forget_bias.reshape(1, D_HEADS).astype(F32), LANES)
    cum = fox_cumsum(proj_b, blk_fd, fbias, batch, seq)[:, :, :D_HEADS]
    cum_t = cum.transpose(0, 2, 1)
    hd = 4
    d0 = (qk_w + c_kw) // (hd * HEAD_DIM)
    od = flash_attention(pa3, pa3, pa3, n_heads=D_HEADS, hq=hd, hk=hd, dk=HEAD_DIM, dv=HEAD_DIM,
                         q_blk=d0, k_blk=d0 + d_w // (hd * HEAD_DIM), v_blk=d0 + 2 * d_w // (hd * HEAD_DIM),
                         mode="causal", fox=(cum_t[..., None], cum_t[:, :, None, :]))
    return matmul_cat(oc.reshape(T, -1), od.reshape(T, -1), w_out.astype(BF16), F32)


def memory_cross_attention(h, hb, mem_b, batch, seq, w_q, w_kv, w_o, ln_g, ln_b):
    T, D = hb.shape
    mem_w = MEM_HEADS * MEM_HD
    q_scale = MEM_HD ** -0.5 * LOG2E
    q = matmul(hb, (w_q * q_scale).astype(BF16), BF16)
    kv = matmul(mem_b.reshape(-1, D), w_kv.astype(BF16), BF16)
    kv3 = kv.reshape(batch, -1, 2 * mem_w)
    o = flash_attention(q.reshape(batch, seq, mem_w), kv3, kv3, n_heads=MEM_HEADS, hq=MEM_HEADS, hk=MEM_HEADS,
                        dk=MEM_HD, dv=MEM_HD, q_blk=0, k_blk=0, v_blk=1, mode="none", kb=kv3.shape[1])
    return matmul_deepnorm_ln(o.reshape(T, mem_w), w_o.astype(BF16), h, ln_g, ln_b)


def peer_ffn(ht, w_q, sub_keys, u_tabs, v_tabs, layer):
    qt = matmul(w_q.T.astype(BF16), ht, F32)
    sk = sub_keys.reshape(2 * PEER_HEADS, N_KEYS, PEER_DKEY // 2).astype(BF16)
    s_t, stats = peer_route(qt, sk)
    u_b, v_b = cast_layer_pair(u_tabs, v_tabs, layer)
    return peer_dense(ht, u_b, v_b, s_t, stats)


def kernel(x, mem, ab_w_in, a_rel_bias, b_q_norm, b_w_uq, b_kv_norm, b_w_ukv, ab_w_out, cd_w_in, d_forget_bias,
           cd_w_out, mem_w_q, mem_w_kv, mem_w_o, peer_w_q, peer_sub_keys, peer_u, peer_v, ln_g, ln_b):
    batch, seq, d_model = x.shape
    h = x.reshape(batch * seq, d_model)
    hb = h.astype(BF16)
    mem_b = mem.astype(BF16)
    for layer in range(DEPTH):
        j = layer // 2
        if layer % 2 == 0:
            y = mixer_ab(hb, batch, seq, ab_w_in[j], a_rel_bias[j], b_q_norm[j], b_w_uq[j], b_kv_norm[j],
                         b_w_ukv[j], ab_w_out[j])
        else:
            y = mixer_cd(hb, batch, seq, cd_w_in[j], d_forget_bias[j], cd_w_out[j])
        h, hb = deepnorm_ln(h, y, ln_g[layer, 0], ln_b[layer, 0])
        h, ht = memory_cross_attention(h, hb, mem_b, batch, seq, mem_w_q[layer], mem_w_kv[layer], mem_w_o[layer],
                                       ln_g[layer, 1], ln_b[layer, 1])
        y = peer_ffn(ht, peer_w_q[layer], peer_sub_keys[layer], peer_u, peer_v, layer)
        h, hb = deepnorm_ln(h, y, ln_g[layer, 2], ln_b[layer, 2])
    return h.reshape(batch, seq, d_model)
```

```python
import functools

import jax
import jax.numpy as jnp
import numpy as np
from jax import lax
from jax.experimental import pallas as pl
from jax.experimental.pallas import tpu as pltpu

F32 = jnp.float32
BF16 = jnp.bfloat16

CHUNK = 64
HEAD_DIM = 128
A_HEADS = 16
LEFT_CHUNKS = 8
REL_CLIP = 256
B_HEADS = 16
MLA_Q_LORA = 1536
MLA_KV_LORA = 512
MLA_NOPE = 128
MLA_ROPE = 64
MLA_V = 128
C_HEADS = 16
C_KV_HEADS = 4
IDX_HEADS = 32
IDX_HD = 64
DSA_TOPK_MAX = 256
D_HEADS = 16
MEM_HEADS = 4
MEM_HD = 128
PEER_HEADS = 8
PEER_DKEY = 256
N_KEYS = 128
PEER_TOPK = 16
ROPE_THETA = 10000.0
LN_EPS = 1e-5
RMS_EPS = 1e-6
DEPTH = 2
DEEPNORM_ALPHA = (2 * DEPTH) ** 0.25

LANES = 128
V7X_VMEM_LIMIT = 56 * 1024 * 1024
MASK_NEG = -1e30
LOG2E = 1.4426950408889634


def _params(sem):
    return pltpu.CompilerParams(dimension_semantics=sem, vmem_limit_bytes=V7X_VMEM_LIMIT)


def _round_up(n, m):
    return (n + m - 1) // m * m


def _mm_kernel(a_ref, w_ref, o_ref):
    o_ref[...] = jnp.dot(a_ref[...], w_ref[...], preferred_element_type=F32).astype(o_ref.dtype)


def matmul(a, w, out_dtype, tm=1024, tn=1024):
    M, K = a.shape
    N = w.shape[1]
    tm = min(tm, M)
    tn = min(tn, N)
    if N % tn:
        tn //= 2
    assert M % tm == 0 and N % tn == 0, (M, N, tm, tn)
    return pl.pallas_call(
        _mm_kernel,
        grid=(M // tm, N // tn),
        in_specs=[pl.BlockSpec((tm, K), lambda i, j: (i, 0)),
                  pl.BlockSpec((K, tn), lambda i, j: (0, j))],
        out_specs=pl.BlockSpec((tm, tn), lambda i, j: (i, j)),
        out_shape=jax.ShapeDtypeStruct((M, N), out_dtype),
        compiler_params=_params(("parallel", "arbitrary")),
        name="matmul",
    )(a, w)


def _mm_rope_kernel(a_ref, w_ref, c_ref, sa_ref, sb_ref, o_ref, *, group, half, rope_blocks):
    def product():
        return jnp.dot(a_ref[...], w_ref[...], preferred_element_type=F32)

    def rotated():
        acc = product()
        c, sa, sb = c_ref[...], sa_ref[...], sb_ref[...]
        for g in range(acc.shape[1] // group):
            x = acc[:, g * group:(g + 1) * group]
            out = x * c + pltpu.roll(x, group - half, 1) * sa + pltpu.roll(x, half, 1) * sb
            o_ref[:, g * group:(g + 1) * group] = out.astype(o_ref.dtype)

    if rope_blocks is None:
        rotated()
    else:
        pl.when(pl.program_id(1) < rope_blocks)(rotated)

        @pl.when(pl.program_id(1) >= rope_blocks)
        def _():
            o_ref[...] = product().astype(o_ref.dtype)


def matmul_rope(a, w, tables, seq, group, half, rope_cols=None, tm=1024, tn=512):
    M, K = a.shape
    N = w.shape[1]
    tm = min(tm, seq)
    assert M % tm == 0 and seq % tm == 0 and N % tn == 0 and tn % group == 0
    assert rope_cols is None or rope_cols % tn == 0
    nsb = seq // tm
    tab = pl.BlockSpec((tm, group), lambda i, j: (i % nsb, 0))
    return pl.pallas_call(
        functools.partial(_mm_rope_kernel, group=group, half=half,
                          rope_blocks=None if rope_cols is None else rope_cols // tn),
        grid=(M // tm, N // tn),
        in_specs=[pl.BlockSpec((tm, K), lambda i, j: (i, 0)),
                  pl.BlockSpec((K, tn), lambda i, j: (0, j)), tab, tab, tab],
        out_specs=pl.BlockSpec((tm, tn), lambda i, j: (i, j)),
        out_shape=jax.ShapeDtypeStruct((M, N), BF16),
        compiler_params=_params(("parallel", "arbitrary")),
        name="matmul_rope",
    )(a, w, *tables)


def _mm2_kernel(a1_ref, a2_ref, w_ref, o_ref):
    a = jnp.concatenate([a1_ref[...], a2_ref[...]], axis=1)
    o_ref[...] = jnp.dot(a, w_ref[...], preferred_element_type=F32).astype(o_ref.dtype)


def matmul_cat(a1, a2, w, out_dtype, tm=1024, tn=512):
    M, K1 = a1.shape
    K2 = a2.shape[1]
    N = w.shape[1]
    tm = min(tm, M)
    tn = min(tn, N)
    assert M % tm == 0 and N % tn == 0 and w.shape[0] == K1 + K2
    return pl.pallas_call(
        _mm2_kernel,
        grid=(M // tm, N // tn),
        in_specs=[pl.BlockSpec((tm, K1), lambda i, j: (i, 0)),
                  pl.BlockSpec((tm, K2), lambda i, j: (i, 0)),
                  pl.BlockSpec((K1 + K2, tn), lambda i, j: (0, j))],
        out_specs=pl.BlockSpec((tm, tn), lambda i, j: (i, j)),
        out_shape=jax.ShapeDtypeStruct((M, N), out_dtype),
        compiler_params=_params(("parallel", "arbitrary")),
        name="matmul_cat",
    )(a1, a2, w)


def _cast_kernel(u_ref, v_ref, ou_ref, ov_ref):
    ou_ref[...] = u_ref[...].astype(ou_ref.dtype)
    ov_ref[...] = v_ref[...].astype(ov_ref.dtype)


def cast_layer_pair(u, v, layer, tr=256):
    _, R, C = u.shape
    src = pl.BlockSpec((None, tr, C), lambda i: (layer, i, 0))
    dst = pl.BlockSpec((tr, C), lambda i: (i, 0))
    return pl.pallas_call(
        _cast_kernel,
        grid=(R // tr,),
        in_specs=[src, src],
        out_specs=[dst, dst],
        out_shape=[jax.ShapeDtypeStruct((R, C), BF16)] * 2,
        compiler_params=_params(("parallel",)),
        name="cast_layer_pair",
    )(u, v)


def _ln_kernel(h_ref, y_ref, g_ref, b_ref, o_ref, ob_ref):
    z = DEEPNORM_ALPHA * h_ref[...] + y_ref[...]
    zc = z - jnp.mean(z, axis=-1, keepdims=True)
    var = jnp.mean(zc * zc, axis=-1, keepdims=True)
    out = zc * lax.rsqrt(var + LN_EPS) * g_ref[...] + b_ref[...]
    o_ref[...] = out
    ob_ref[...] = out.astype(BF16)


def deepnorm_ln(h, y, g, b, tb=256):
    T, D = h.shape
    row = pl.BlockSpec((tb, D), lambda i: (i, 0))
    vec = pl.BlockSpec((1, D), lambda i: (0, 0))
    return pl.pallas_call(
        _ln_kernel,
        grid=(T // tb,),
        in_specs=[row, row, vec, vec],
        out_specs=[row, row],
        out_shape=[jax.ShapeDtypeStruct((T, D), F32), jax.ShapeDtypeStruct((T, D), BF16)],
        compiler_params=_params(("parallel",)),
        name="deepnorm_ln",
    )(h, y, g.reshape(1, D), b.reshape(1, D))


def _mm_ln_kernel(a_ref, w_ref, h_ref, g_ref, b_ref, o_ref, obt_ref):
    z = DEEPNORM_ALPHA * h_ref[...] + jnp.dot(a_ref[...], w_ref[...], preferred_element_type=F32)
    zc = z - jnp.mean(z, axis=-1, keepdims=True)
    var = jnp.mean(zc * zc, axis=-1, keepdims=True)
    out = zc * lax.rsqrt(var + LN_EPS) * g_ref[...] + b_ref[...]
    o_ref[...] = out
    obt_ref[...] = out.T.astype(BF16)


def matmul_deepnorm_ln(a, w, h, g, b, tb=256):
    T, D = h.shape
    K = a.shape[1]
    row = pl.BlockSpec((tb, D), lambda i: (i, 0))
    vec = pl.BlockSpec((1, D), lambda i: (0, 0))
    return pl.pallas_call(
        _mm_ln_kernel,
        grid=(T // tb,),
        in_specs=[pl.BlockSpec((tb, K), lambda i: (i, 0)), pl.BlockSpec((K, D), lambda i: (0, 0)), row, vec, vec],
        out_specs=[row, pl.BlockSpec((D, tb), lambda i: (0, i))],
        out_shape=[jax.ShapeDtypeStruct((T, D), F32), jax.ShapeDtypeStruct((D, T), BF16)],
        compiler_params=_params(("parallel",)),
        name="matmul_deepnorm_ln",
    )(a, w, h, g.reshape(1, D), b.reshape(1, D))


def _rms_kernel(x_ref, g_ref, o_ref):
    x = x_ref[...].astype(F32)
    o_ref[...] = (x * lax.rsqrt(jnp.mean(x * x, axis=-1, keepdims=True) + RMS_EPS) * g_ref[...]).astype(o_ref.dtype)


def rms_norm_cols(x, col_block, width, g, tb=512):
    T = x.shape[0]
    return pl.pallas_call(
        _rms_kernel,
        grid=(T // tb,),
        in_specs=[pl.BlockSpec((tb, width), lambda i: (i, col_block)),
                  pl.BlockSpec((1, width), lambda i: (0, 0))],
        out_specs=pl.BlockSpec((tb, width), lambda i: (i, 0)),
        out_shape=jax.ShapeDtypeStruct((T, width), BF16),
        compiler_params=_params(("parallel",)),
        name="rms_norm",
    )(x, g.reshape(1, width))


def rope_tables(seq, dim, group, offset=0, reps=1, scale=1.0):
    half = dim // 2
    inv = ROPE_THETA ** (-jnp.arange(half, dtype=F32) * 2.0 / dim)
    ang = jnp.arange(seq, dtype=F32)[:, None] * inv[None, :]
    cos, sin = jnp.cos(ang), jnp.sin(ang)
    zeros_tail = jnp.zeros((seq, group - offset - dim * reps), F32)
    head = jnp.ones((seq, offset), F32)
    zhead = jnp.zeros((seq, offset), F32)
    zhalf = jnp.zeros((seq, half), F32)
    c = jnp.concatenate([head] + [cos, cos] * reps + [zeros_tail], axis=1)
    sa = jnp.concatenate([zhead] + [-sin, zhalf] * reps + [zeros_tail], axis=1)
    sb = jnp.concatenate([zhead] + [zhalf, sin] * reps + [zeros_tail], axis=1)
    return c * scale, sa * scale, sb * scale


def _rope_kernel(x_ref, c_ref, sa_ref, sb_ref, o_ref, *, groups, group, half):
    c, sa, sb = c_ref[...], sa_ref[...], sb_ref[...]
    for g in range(groups):
        x = x_ref[:, g * group:(g + 1) * group].astype(F32)
        out = x * c + pltpu.roll(x, group - half, 1) * sa + pltpu.roll(x, half, 1) * sb
        o_ref[:, g * group:(g + 1) * group] = out.astype(o_ref.dtype)


def rope_cols(x, col_block, groups, group, half, tables, seq, tb=256):
    T = x.shape[0]
    W = groups * group
    nsb = seq // tb
    tab = pl.BlockSpec((tb, group), lambda i: (i % nsb, 0))
    return pl.pallas_call(
        functools.partial(_rope_kernel, groups=groups, group=group, half=half),
        grid=(T // tb,),
        in_specs=[pl.BlockSpec((tb, W), lambda i: (i, col_block)), tab, tab, tab],
        out_specs=pl.BlockSpec((tb, W), lambda i: (i, 0)),
        out_shape=jax.ShapeDtypeStruct((T, W), BF16),
        compiler_params=_params(("parallel",)),
        name="rope",
    )(x, *tables)


def _mla_kv_kernel(kv_ref, kr_ref, c_ref, sa_ref, sb_ref, k_ref, v_ref, *, heads, half):
    kr = kr_ref[...].astype(F32)
    kr = kr * c_ref[...] + pltpu.roll(kr, LANES - half, 1) * sa_ref[...] + pltpu.roll(kr, half, 1) * sb_ref[...]
    kr = kr.astype(k_ref.dtype)
    for h in range(heads):
        k_ref[:, 2 * h * LANES:(2 * h + 1) * LANES] = kv_ref[:, 2 * h * LANES:(2 * h + 1) * LANES]
        k_ref[:, (2 * h + 1) * LANES:(2 * h + 2) * LANES] = kr
        v_ref[:, h * LANES:(h + 1) * LANES] = kv_ref[:, (2 * h + 1) * LANES:(2 * h + 2) * LANES]


def mla_assemble_kv(kv, cproj, kr_col_block, tables, seq, tb=256):
    T, W = kv.shape
    nsb = seq // tb
    tab = pl.BlockSpec((tb, LANES), lambda i: (i % nsb, 0))
    return pl.pallas_call(
        functools.partial(_mla_kv_kernel, heads=B_HEADS, half=MLA_ROPE // 2),
        grid=(T // tb,),
        in_specs=[pl.BlockSpec((tb, W), lambda i: (i, 0)),
                  pl.BlockSpec((tb, LANES), lambda i: (i, kr_col_block)), tab, tab, tab],
        out_specs=[pl.BlockSpec((tb, W), lambda i: (i, 0)), pl.BlockSpec((tb, W // 2), lambda i: (i, 0))],
        out_shape=[jax.ShapeDtypeStruct((T, W), BF16), jax.ShapeDtypeStruct((T, W // 2), BF16)],
        compiler_params=_params(("parallel",)),
        name="mla_assemble_kv",
    )(kv, cproj, *tables)


FLASH_RB = 128


def _flash_kernel(*refs, hq, hk, dk, dv, qb, kb, mode, fox, q_axis, n_kv_blocks):
    it = iter(refs)
    q_ref, k_ref, v_ref = next(it), next(it), next(it)
    ct_ref = cs_ref = bias_ref = None
    if fox:
        ct_ref, cs_ref = next(it), next(it)
    if mode == "bias":
        bias_ref = next(it)
    o_ref, m_ref, l_ref, acc_ref, p_ref, alpha_ref = next(it), next(it), next(it), next(it), next(it), next(it)
    ct_lanes_ref = next(it) if fox else None

    qi = pl.program_id(q_axis)
    m_ref[...] = jnp.full(m_ref.shape, MASK_NEG, F32)
    l_ref[...] = jnp.zeros(l_ref.shape, F32)
    acc_ref[...] = jnp.zeros(acc_ref.shape, F32)
    if fox:
        for h in range(hq):
            ct_lanes_ref[h] = jnp.broadcast_to(ct_ref[0, h] * LOG2E, (qb, LANES))

    n_sub = qb // FLASH_RB
    per_kv = kb // FLASH_RB
    rows = lax.broadcasted_iota(jnp.int32, (FLASH_RB, kb), 0)
    cols = lax.broadcasted_iota(jnp.int32, (FLASH_RB, kb), 1)

    def diag_mask(r):
        row_in_kv = rows + (r % per_kv) * FLASH_RB
        if mode == "causal":
            return cols <= row_in_kv
        if mode == "chunk":
            return (cols // CHUNK) <= (row_in_kv // CHUNK)
        return None

    def step(j, visible):
        ks = pl.multiple_of(j * kb, kb)
        for h in range(hq):
            g = h if hk == hq else 0
            k = k_ref[0, pl.ds(ks, kb), g * dk:(g + 1) * dk]
            v = v_ref[0, pl.ds(ks, kb), g * dv:(g + 1) * dv]
            s_all = lax.dot_general(q_ref[0, :, h * dk:(h + 1) * dk], k, (((1,), (1,)), ((), ())),
                                    preferred_element_type=F32)
            cs = cs_ref[0, h, :, pl.ds(ks, kb)] * LOG2E if fox else None
            for r in range(n_sub):
                rs = slice(r * FLASH_RB, (r + 1) * FLASH_RB)
                if visible[r] == "none":
                    p_ref[h, rs, :] = jnp.zeros((FLASH_RB, kb), p_ref.dtype)
                    alpha_ref[h, rs, :] = jnp.ones((FLASH_RB, LANES), F32)
                    continue
                s = s_all[rs]
                if bias_ref is not None:
                    s = s + bias_ref[0, rs, pl.ds(ks, kb)].astype(F32)
                blocks = [s[:, c * LANES:(c + 1) * LANES] for c in range(kb // LANES)]
                if fox:
                    ct = ct_lanes_ref[h, rs]
                    blocks = [blk + ct - cs[:, c * LANES:(c + 1) * LANES] for c, blk in enumerate(blocks)]
                if visible[r] == "diag" and mode in ("causal", "chunk"):
                    mask = diag_mask(r)
                    blocks = [jnp.where(mask[:, c * LANES:(c + 1) * LANES], blk, MASK_NEG)
                              for c, blk in enumerate(blocks)]
                blk_max = blocks[0]
                for blk in blocks[1:]:
                    blk_max = jnp.maximum(blk_max, blk)
                m_prev = m_ref[h, rs]
                m_new = jnp.maximum(m_prev, jnp.max(blk_max, axis=1, keepdims=True))
                alpha = jnp.exp2(m_prev - m_new)
                p_blocks = [jnp.exp2(blk - m_new) for blk in blocks]
                p_lanes = p_blocks[0]
                for pb in p_blocks[1:]:
                    p_lanes = p_lanes + pb
                l_ref[h, rs] = alpha * l_ref[h, rs] + p_lanes
                for c, pb in enumerate(p_blocks):
                    p_ref[h, rs, c * LANES:(c + 1) * LANES] = pb.astype(p_ref.dtype)
                alpha_ref[h, rs, :] = alpha
                m_ref[h, rs] = m_new
            pv = jnp.dot(p_ref[h], v, preferred_element_type=F32)
            acc_ref[h] = alpha_ref[h] * acc_ref[h] + pv

    def body(j, carry):
        step(j, ["full"] * n_sub)
        return carry

    if mode == "none":
        lax.fori_loop(0, n_kv_blocks, body, 0)
    else:
        kv_per_q = qb // kb
        lax.fori_loop(0, qi * kv_per_q, body, 0)
        for t in range(kv_per_q):
            kinds = ["full" if r // per_kv > t else "diag" if r // per_kv == t else "none" for r in range(n_sub)]
            step(qi * kv_per_q + t, kinds)

    for h in range(hq):
        l = jnp.sum(l_ref[h], axis=1, keepdims=True)
        o_ref[0, :, h * dv:(h + 1) * dv] = (acc_ref[h] / l).astype(o_ref.dtype)


def flash_attention(q, k, v, *, n_heads, hq, hk, dk, dv, q_blk, k_blk, v_blk, mode,
                    qb=512, kb=512, fox=None, bias=None):
    B, Sq = q.shape[0], q.shape[1]
    Skv = k.shape[1]
    kb = min(kb, Skv)
    nq, ng = Sq // qb, n_heads // hq
    assert hk in (hq, 1) and qb % kb == 0 and kb % FLASH_RB == 0
    if mode in ("causal", "chunk", "bias"):
        assert Sq == Skv
    if mode != "bias":
        grid = (B, ng, nq)
        q_axis = 2

        def spec(shape, fn):
            return pl.BlockSpec(shape, lambda b, g, i: fn(b, g, i))
    else:
        grid = (B, nq, ng)
        q_axis = 1

        def spec(shape, fn):
            return pl.BlockSpec(shape, lambda b, i, g: fn(b, g, i))

    in_specs = [
        spec((1, qb, hq * dk), lambda b, g, i: (b, i, q_blk + g)),
        spec((1, Skv, hk * dk), lambda b, g, i: (b, 0, k_blk + g)),
        spec((1, Skv, hk * dv), lambda b, g, i: (b, 0, v_blk + g)),
    ]
    args = [q, k, v]
    if fox is not None:
        ct, cs = fox
        in_specs += [spec((1, hq, qb, 1), lambda b, g, i: (b, g, i, 0)),
                     spec((1, hq, 1, Skv), lambda b, g, i: (b, g, 0, 0))]
        args += [ct, cs]
    if mode == "bias":
        in_specs.append(spec((1, qb, Skv), lambda b, g, i: (b, i, 0)))
        args.append(bias)
    assert dv == LANES
    stat = pltpu.VMEM((hq, qb, LANES), F32)
    kern = functools.partial(_flash_kernel, hq=hq, hk=hk, dk=dk, dv=dv, qb=qb, kb=kb, mode=mode,
                             fox=fox is not None, q_axis=q_axis, n_kv_blocks=Skv // kb)
    return pl.pallas_call(
        kern,
        grid=grid,
        in_specs=in_specs,
        out_specs=spec((1, qb, hq * dv), lambda b, g, i: (b, i, g)),
        out_shape=jax.ShapeDtypeStruct((B, Sq, n_heads * dv), BF16),
        scratch_shapes=[stat, stat, pltpu.VMEM((hq, qb, dv), F32), pltpu.VMEM((hq, qb, kb), BF16), stat]
        + ([stat] if fox is not None else []),
        compiler_params=_params(("parallel", "parallel", "arbitrary")),
        name="flash_" + mode,
    )(*args)


BAND_QB = 4 * CHUNK
BAND_KBLOCKS = (LEFT_CHUNKS * CHUNK) // BAND_QB + 1
BAND_W = BAND_KBLOCKS * BAND_QB


def band_bias_tiles(rel_bias):
    pad = LEFT_CHUNKS * CHUNK
    period = BAND_QB + BAND_W
    m = np.arange(period)
    j_minus_i = np.where(m < period - BAND_QB, m, m - period)
    r = rel_bias[:, np.clip(pad - j_minus_i, -REL_CLIP, REL_CLIP) + REL_CLIP].astype(F32)
    skew = jnp.tile(r, (1, BAND_QB))[:, :BAND_QB * (period - 1)].reshape(-1, BAND_QB, period - 1)
    i = np.arange(BAND_QB)[:, None]
    j = np.arange(BAND_W)[None, :]
    chunk_diff = i // CHUNK + LEFT_CHUNKS - j // CHUNK
    in_band = (chunk_diff >= 0) & (chunk_diff <= LEFT_CHUNKS)
    return jnp.where(in_band[None], skew[:, :, :BAND_W] * LOG2E, MASK_NEG)


def _band_kernel(q_ref, k_ref, v_ref, bias_ref, o_ref, s_ref, p_ref, l_ref, *, hb):
    qi = pl.program_id(2)
    for h in range(hb):
        lanes = slice(h * HEAD_DIM, (h + 1) * HEAD_DIM)
        q = q_ref[0, :, lanes]
        v_parts = []
        for jb in range(BAND_KBLOCKS):
            kblk = qi - (BAND_KBLOCKS - 1) + jb
            ks = pl.multiple_of(jnp.maximum(kblk, 0) * BAND_QB, BAND_QB)
            k = k_ref[0, pl.ds(ks, BAND_QB), lanes]
            v_parts.append(v_ref[0, pl.ds(ks, BAND_QB), lanes])
            s = lax.dot_general(q, k, (((1,), (1,)), ((), ())), preferred_element_type=F32)
            s = s + bias_ref[h, :, jb * BAND_QB:(jb + 1) * BAND_QB]
            s_ref[h, :, jb * BAND_QB:(jb + 1) * BAND_QB] = jnp.where(kblk >= 0, s, MASK_NEG)
        for r in range(BAND_QB // CHUNK):
            rs = slice(r * CHUNK, (r + 1) * CHUNK)
            blocks = [s_ref[h, rs, c * LANES:(c + 1) * LANES] for c in range(BAND_W // LANES)]
            blk_max = blocks[0]
            for blk in blocks[1:]:
                blk_max = jnp.maximum(blk_max, blk)
            m = jnp.broadcast_to(jnp.max(blk_max, axis=1, keepdims=True), (CHUNK, LANES))
            p_lanes = jnp.zeros((CHUNK, LANES), F32)
            for c, blk in enumerate(blocks):
                p = jnp.exp2(blk - m)
                p_lanes = p_lanes + p
                p_ref[h, rs, c * LANES:(c + 1) * LANES] = p.astype(p_ref.dtype)
            l_ref[h, rs, :] = p_lanes
        acc = jnp.zeros((BAND_QB, HEAD_DIM), F32)
        for jb in range(BAND_KBLOCKS):
            acc = acc + jnp.dot(p_ref[h, :, jb * BAND_QB:(jb + 1) * BAND_QB], v_parts[jb], preferred_element_type=F32)
        l = jnp.sum(l_ref[h], axis=1, keepdims=True)
        o_ref[0, :, lanes] = (acc / l).astype(o_ref.dtype)


def band_attention(qkv, bias_tiles, hb=4):
    B, S, _ = qkv.shape
    H = A_HEADS
    ng = H // hb
    return pl.pallas_call(
        functools.partial(_band_kernel, hb=hb),
        grid=(B, ng, S // BAND_QB),
        in_specs=[pl.BlockSpec((1, BAND_QB, hb * HEAD_DIM), lambda b, g, i: (b, i, g)),
                  pl.BlockSpec((1, S, hb * HEAD_DIM), lambda b, g, i: (b, 0, ng + g)),
                  pl.BlockSpec((1, S, hb * HEAD_DIM), lambda b, g, i: (b, 0, 2 * ng + g)),
                  pl.BlockSpec((hb, BAND_QB, BAND_W), lambda b, g, i: (g, 0, 0))],
        out_specs=pl.BlockSpec((1, BAND_QB, hb * HEAD_DIM), lambda b, g, i: (b, i, g)),
        out_shape=jax.ShapeDtypeStruct((B, S, H * HEAD_DIM), BF16),
        scratch_shapes=[pltpu.VMEM((hb, BAND_QB, BAND_W), F32), pltpu.VMEM((hb, BAND_QB, BAND_W), BF16),
                        pltpu.VMEM((hb, BAND_QB, LANES), F32)],
        compiler_params=_params(("parallel", "parallel", "arbitrary")),
        name="band_attention",
    )(qkv, qkv, qkv, bias_tiles)


def _fox_cumsum_kernel(f_ref, b_ref, o_ref, carry_ref, *, cb):
    @pl.when(pl.program_id(1) == 0)
    def _():
        carry_ref[...] = jnp.zeros(carry_ref.shape, F32)

    x = f_ref[0].astype(F32) + b_ref[...]
    log_f = jnp.minimum(x, 0.0) - jnp.log1p(jnp.exp(-jnp.abs(x)))
    r = lax.broadcasted_iota(jnp.int32, (cb, cb), 0)
    c = lax.broadcasted_iota(jnp.int32, (cb, cb), 1)
    tri = jnp.where(c <= r, 1.0, 0.0).astype(F32)
    cum = jnp.dot(tri, log_f, preferred_element_type=F32, precision=lax.Precision.HIGHEST) + carry_ref[...]
    o_ref[0] = cum
    carry_ref[...] = cum[cb - 1:cb, :]


def fox_cumsum(proj, col_block, bias_row, batch, seq, cb=256):
    x = proj.reshape(batch, seq, proj.shape[-1])
    return pl.pallas_call(
        functools.partial(_fox_cumsum_kernel, cb=cb),
        grid=(batch, seq // cb),
        in_specs=[pl.BlockSpec((1, cb, LANES), lambda b, i: (b, i, col_block)),
                  pl.BlockSpec((1, LANES), lambda b, i: (0, 0))],
        out_specs=pl.BlockSpec((1, cb, LANES), lambda b, i: (b, i, 0)),
        out_shape=jax.ShapeDtypeStruct((batch, seq, LANES), F32),
        scratch_shapes=[pltpu.VMEM((1, LANES), F32)],
        compiler_params=_params(("parallel", "arbitrary")),
        name="fox_cumsum",
    )(x, bias_row)


IDX_QB = 128
IDX_KC = 512


def _sortable_key(x):
    bits = pltpu.bitcast(x, jnp.int32)
    return bits ^ ((bits >> 31) & jnp.int32(0x7FFFFFFF))


def _indexer_kernel(ki_ref, qit_ref, w_ref, o_ref, key_ref, *, seq, topk):
    qi = pl.program_id(1)
    q0 = qi * IDX_QB
    n_chunks = (q0 + IDX_QB + IDX_KC - 1) // IDX_KC
    qchunk = (q0 + lax.broadcasted_iota(jnp.int32, (1, IDX_QB), 1)) // CHUNK
    int_min = jnp.int32(-2 ** 31)
    w_scale = IDX_HEADS ** -0.5 * IDX_HD ** -0.5

    def score_chunk(c, carry):
        ks = pl.multiple_of(c * IDX_KC, IDX_KC)
        ki = ki_ref[0, pl.ds(ks, IDX_KC), :][:, :IDX_HD]
        acc = jnp.zeros((IDX_KC, IDX_QB), F32)
        for hp in range(IDX_HEADS // 2):
            t = jnp.dot(ki, qit_ref[0, 0, :, hp * 2 * IDX_QB:(hp + 1) * 2 * IDX_QB], preferred_element_type=F32)
            w = w_ref[0, 0, :, hp * 2 * IDX_QB:(hp + 1) * 2 * IDX_QB] * w_scale
            t = jnp.maximum(t, 0.0) * w
            acc = acc + t[:, :IDX_QB] + t[:, IDX_QB:]
        kchunk = (ks + lax.broadcasted_iota(jnp.int32, (IDX_KC, 1), 0)) // CHUNK
        acc = jnp.where(kchunk <= qchunk, acc, -jnp.inf)
        key_ref[pl.ds(ks, IDX_KC), :] = _sortable_key(acc)
        return carry

    lax.fori_loop(0, n_chunks, score_chunk, 0)

    def count_ge(cand):
        def body(c, acc):
            ks = pl.multiple_of(c * IDX_KC, IDX_KC)
            blk = key_ref[pl.ds(ks, IDX_KC), :]
            hit = jnp.where(blk >= cand, 1, 0).astype(jnp.int32)
            return acc + jnp.sum(hit.reshape(IDX_KC // 8, 8, IDX_QB), axis=0)
        acc = lax.fori_loop(0, n_chunks, body, jnp.zeros((8, IDX_QB), jnp.int32))
        return jnp.sum(acc, axis=0, keepdims=True)

    def bit_step(i, ans):
        bit = lax.shift_left(jnp.int32(1), 31 - i)
        cand = ans | bit
        cnt = count_ge(cand ^ int_min)
        return jnp.where(cnt >= topk, cand, ans)

    ans = lax.fori_loop(0, 32, bit_step, jnp.zeros((1, IDX_QB), jnp.int32))
    thr = ans ^ int_min
    neg_inf_key = _sortable_key(jnp.full((1, IDX_QB), -jnp.inf, F32))

    def emit(c, carry):
        ks = pl.multiple_of(c * IDX_KC, IDX_KC)
        blk = key_ref[pl.ds(ks, IDX_KC), :]
        sel = (blk >= thr) & (blk > neg_inf_key)
        bias_t = jnp.where(sel, 0.0, MASK_NEG).astype(F32)
        o_ref[0, :, pl.ds(ks, IDX_KC)] = bias_t.T.astype(o_ref.dtype)
        return carry

    lax.fori_loop(0, n_chunks, emit, 0)

    def fill(c, carry):
        ks = pl.multiple_of(c * IDX_KC, IDX_KC)
        o_ref[0, :, pl.ds(ks, IDX_KC)] = jnp.full((IDX_QB, IDX_KC), MASK_NEG, o_ref.dtype)
        return carry

    lax.fori_loop(n_chunks, seq // IDX_KC, fill, 0)


def dsa_selection_bias(ki, qit, wt, topk):
    B, S, _ = ki.shape
    nq = S // IDX_QB
    return pl.pallas_call(
        functools.partial(_indexer_kernel, seq=S, topk=topk),
        grid=(B, nq),
        in_specs=[pl.BlockSpec((1, S, LANES), lambda b, i: (b, 0, 0)),
                  pl.BlockSpec((1, 1, IDX_HD, IDX_HEADS * IDX_QB), lambda b, i: (b, i, 0, 0)),
                  pl.BlockSpec((1, 1, 1, IDX_HEADS * IDX_QB), lambda b, i: (b, i, 0, 0))],
        out_specs=pl.BlockSpec((1, IDX_QB, S), lambda b, i: (b, i, 0)),
        out_shape=jax.ShapeDtypeStruct((B, S, S), BF16),
        scratch_shapes=[pltpu.VMEM((S, IDX_QB), jnp.int32)],
        compiler_params=_params(("parallel", "arbitrary")),
        name="dsa_indexer",
    )(ki, qit, wt)


PEER_TB_ROUTE = 128


def _peer_route_kernel(qt_ref, sk_ref, cut_ref, e1_ref, s2_ref, e2_ref, s1_ref, top_ref):
    half = PEER_DKEY // 2
    tb = PEER_TB_ROUTE
    for hp in range(2 * PEER_HEADS):
        q = qt_ref[hp * half:(hp + 1) * half, :].astype(BF16)
        s = jnp.dot(sk_ref[hp], q, preferred_element_type=F32)
        if hp % 2 == 0:
            s1_ref[hp // 2] = s
        else:
            s2_ref[hp // 2] = s

    def head_tops(h, carry):
        def extract_pair(i, c):
            xa, xb = c
            ma = jnp.max(xa, axis=0, keepdims=True)
            mb = jnp.max(xb, axis=0, keepdims=True)
            top_ref[2 * h, pl.ds(i, 1), :] = ma
            top_ref[2 * h + 1, pl.ds(i, 1), :] = mb
            return jnp.where(xa == ma, -jnp.inf, xa), jnp.where(xb == mb, -jnp.inf, xb)

        lax.fori_loop(0, PEER_TOPK, extract_pair, (s1_ref[h], s2_ref[h]))
        return carry

    lax.fori_loop(0, PEER_HEADS, head_tops, 0)

    def candidates(h):
        a = top_ref[2 * h]
        b = top_ref[2 * h + 1]
        return jnp.concatenate([a[0:1, :] + b] + [a[i:i + 1, :] + b[0:8, :] for i in range(1, 8)]
                               + [a[8:PEER_TOPK, :] + b[0:1, :]], axis=0)

    def extract(i, x, mx, z):
        m = jnp.max(x, axis=0, keepdims=True)
        mx = jnp.where(i == 0, m, mx)
        return jnp.where(x == m, -jnp.inf, x), mx, z + jnp.exp(m - mx), m

    def pair_stats(hh, carry):
        h0, h1 = 2 * hh, 2 * hh + 1

        def extract2(i, c):
            x0, mx0, z0, _, x1, mx1, z1, _ = c
            return extract(i, x0, mx0, z0) + extract(i, x1, mx1, z1)

        zero = jnp.zeros((1, tb), F32)
        res = lax.fori_loop(0, PEER_TOPK, extract2, (candidates(h0), zero, zero, zero, candidates(h1), zero, zero, zero))
        for h, (_, _, z, thr) in ((h0, res[:4]), (h1, res[4:])):
            a = top_ref[2 * h]
            b = top_ref[2 * h + 1]
            s1 = s1_ref[h]
            cut = jnp.full((N_KEYS, tb), jnp.inf, F32)
            for i in range(PEER_TOPK):
                a_i = a[i:i + 1, :]
                cut_i = jnp.min(jnp.where(a_i + b >= thr, b, jnp.inf), axis=0, keepdims=True)
                cut = jnp.where(s1 == a_i, cut_i, cut)
            cut_ref[h] = cut
            e1_ref[h] = jnp.exp(s1 - a[0:1, :]) * (1.0 / z)
            e2_ref[h] = jnp.exp(s2_ref[h] - b[0:1, :])
        return carry

    lax.fori_loop(0, PEER_HEADS // 2, pair_stats, 0)


def peer_route(qt, sub_keys):
    R, T = qt.shape
    tb = PEER_TB_ROUTE
    nsub = 2 * PEER_HEADS
    table = pl.BlockSpec((PEER_HEADS, N_KEYS, tb), lambda i: (0, 0, i))
    return pl.pallas_call(
        _peer_route_kernel,
        grid=(T // tb,),
        in_specs=[pl.BlockSpec((R, tb), lambda i: (0, i)),
                  pl.BlockSpec((nsub, N_KEYS, PEER_DKEY // 2), lambda i: (0, 0, 0))],
        out_specs=[table] * 4,
        out_shape=[jax.ShapeDtypeStruct((PEER_HEADS, N_KEYS, T), F32)] * 4,
        scratch_shapes=[pltpu.VMEM((PEER_HEADS, N_KEYS, tb), F32), pltpu.VMEM((nsub, PEER_TOPK, tb), F32)],
        compiler_params=_params(("parallel",)),
        name="peer_route",
    )(qt, sub_keys)


PEER_TB = 512
PEER_EB = 512


def _gelu_exact(x):
    return 0.5 * x * (1.0 + lax.erf(x * (2.0 ** -0.5)))


def _peer_dense_kernel(xt_ref, u_ref, v_ref, cut_ref, e1_ref, s2_ref, e2_ref, y_ref, ht_ref):
    e = pl.program_id(1)

    @pl.when(e == 0)
    def _():
        y_ref[...] = jnp.zeros(y_ref.shape, F32)

    act = _gelu_exact(jnp.dot(u_ref[...], xt_ref[...], preferred_element_type=F32))
    rows_per_step = PEER_EB // N_KEYS
    for r in range(rows_per_step):
        i1 = e * rows_per_step + r
        gate = jnp.zeros((N_KEYS, PEER_TB), F32)
        for h in range(PEER_HEADS):
            cut = cut_ref[h, pl.ds(i1, 1), :]
            e1 = e1_ref[h, pl.ds(i1, 1), :]
            gate = gate + jnp.where(s2_ref[h] >= cut, e2_ref[h] * e1, 0.0)
        ht_ref[r * N_KEYS:(r + 1) * N_KEYS, :] = (gate * act[r * N_KEYS:(r + 1) * N_KEYS, :]).astype(BF16)
    y_ref[...] += lax.dot_general(ht_ref[...], v_ref[...], (((0,), (0,)), ((), ())), preferred_element_type=F32)


def peer_dense(xt, u, v, tables):
    D, T = xt.shape
    E = u.shape[0]
    once = pl.Buffered(1)
    table = pl.BlockSpec((PEER_HEADS, N_KEYS, PEER_TB), lambda i, e: (0, 0, i), pipeline_mode=once)
    return pl.pallas_call(
        _peer_dense_kernel,
        grid=(T // PEER_TB, E // PEER_EB),
        in_specs=[pl.BlockSpec((D, PEER_TB), lambda i, e: (0, i), pipeline_mode=once),
                  pl.BlockSpec((PEER_EB, D), lambda i, e: (e, 0)),
                  pl.BlockSpec((PEER_EB, D), lambda i, e: (e, 0)),
                  table, table, table, table],
        out_specs=pl.BlockSpec((PEER_TB, D), lambda i, e: (i, 0)),
        out_shape=jax.ShapeDtypeStruct((T, D), F32),
        scratch_shapes=[pltpu.VMEM((PEER_EB, PEER_TB), BF16)],
        compiler_params=_params(("parallel", "arbitrary")),
        name="peer_dense",
    )(xt, u, v, *tables)


def _pad_cols(w, width):
    return jnp.pad(w, ((0, 0), (0, width - w.shape[1])))


def mixer_ab(hb, batch, seq, w_in, rel_bias, q_norm, w_uq, kv_norm, w_ukv, w_out):
    T, D = hb.shape
    a_w = A_HEADS * HEAD_DIM
    qa_scale = HEAD_DIM ** -0.5 * LOG2E
    w_qkv = jnp.concatenate([w_in[:, :a_w] * qa_scale, w_in[:, a_w:3 * a_w]], axis=1).astype(BF16)
    c_cols = MLA_Q_LORA + MLA_KV_LORA + LANES
    w_c = _pad_cols(w_in[:, 3 * a_w:], _round_up(c_cols, 512)).astype(BF16)
    qkv = matmul(hb, w_qkv, BF16)
    cproj = matmul(hb, w_c, F32)

    oa = band_attention(qkv.reshape(batch, seq, 3 * a_w), band_bias_tiles(rel_bias))

    qh = MLA_NOPE + MLA_ROPE
    w_uq_p = jnp.pad(w_uq.reshape(MLA_Q_LORA, B_HEADS, qh), ((0, 0), (0, 0), (0, 2 * LANES - qh)))
    w_uq_p = w_uq_p.reshape(MLA_Q_LORA, B_HEADS * 2 * LANES).astype(BF16)
    cq_n = rms_norm_cols(cproj, 0, MLA_Q_LORA, q_norm)
    ckv_n = rms_norm_cols(cproj, MLA_Q_LORA // MLA_KV_LORA, MLA_KV_LORA, kv_norm)
    q_scale = (MLA_NOPE + MLA_ROPE) ** -0.5 * LOG2E
    q_cat = matmul_rope(cq_n, w_uq_p, rope_tables(seq, MLA_ROPE, 2 * LANES, offset=MLA_NOPE, scale=q_scale),
                        seq, 2 * LANES, MLA_ROPE // 2)
    kv = matmul(ckv_n, w_ukv.astype(BF16), BF16)
    k_cat, v_cat = mla_assemble_kv(kv, cproj, (MLA_Q_LORA + MLA_KV_LORA) // LANES,
                                   rope_tables(seq, MLA_ROPE, LANES), seq)
    ob = flash_attention(q_cat.reshape(batch, seq, -1), k_cat.reshape(batch, seq, -1), v_cat.reshape(batch, seq, -1),
                         n_heads=B_HEADS, hq=4, hk=4, dk=2 * LANES, dv=MLA_V, q_blk=0, k_blk=0, v_blk=0,
                         mode="chunk")
    return matmul_cat(oa.reshape(T, -1), ob.reshape(T, -1), w_out.astype(BF16), F32)


def mixer_cd(hb, batch, seq, w_in, forget_bias, w_out):
    T, D = hb.shape
    c_qw, c_kw, d_w = C_HEADS * HEAD_DIM, C_KV_HEADS * HEAD_DIM, D_HEADS * HEAD_DIM
    i_w = IDX_HEADS * IDX_HD
    offs = np.cumsum([0, c_qw, c_kw, c_kw, i_w, IDX_HD, IDX_HEADS, d_w, d_w, d_w, D_HEADS])
    col = lambda n: w_in[:, offs[n]:offs[n + 1]]
    q_scale = HEAD_DIM ** -0.5 * LOG2E
    w_a = jnp.concatenate([col(0) * q_scale, col(1), col(2), col(6) * q_scale, col(7), col(8)], axis=1).astype(BF16)
    w_b = jnp.concatenate([col(3), _pad_cols(col(4), LANES), _pad_cols(col(5), LANES), _pad_cols(col(9), LANES)], axis=1)
    w_b = _pad_cols(w_b, _round_up(w_b.shape[1], 512)).astype(BF16)
    qk_w = c_qw + c_kw
    proj_a = matmul_rope(hb, w_a, rope_tables(seq, HEAD_DIM, LANES), seq, LANES, HEAD_DIM // 2, rope_cols=qk_w)
    proj_b = matmul(hb, w_b, F32)
    blk_ki, blk_wi, blk_fd = i_w // LANES, i_w // LANES + 1, i_w // LANES + 2

    qi_rot = rope_cols(proj_b, 0, i_w // LANES, LANES, IDX_HD // 2, rope_tables(seq, IDX_HD, LANES, reps=2), seq)
    ki_rot = rope_cols(proj_b, blk_ki, 1, LANES, IDX_HD // 2, rope_tables(seq, IDX_HD, LANES), seq)
    nq = seq // IDX_QB
    qit = qi_rot.reshape(batch, nq, IDX_QB, IDX_HEADS, IDX_HD).transpose(0, 1, 4, 3, 2)
    qit = qit.reshape(batch, nq, IDX_HD, IDX_HEADS * IDX_QB)
    wi = proj_b[:, blk_wi * LANES:blk_wi * LANES + IDX_HEADS]
    wt = wi.reshape(batch, nq, IDX_QB, IDX_HEADS).transpose(0, 1, 3, 2).reshape(batch, nq, 1, IDX_HEADS * IDX_QB)
    sel_bias = dsa_selection_bias(ki_rot.reshape(batch, seq, LANES), qit, wt, min(DSA_TOPK_MAX, seq // 4))
    pa3 = proj_a.reshape(batch, seq, -1)
    rep = C_HEADS // C_KV_HEADS
    oc = flash_attention(pa3, pa3, pa3, n_heads=C_HEADS, hq=rep, hk=1, dk=HEAD_DIM, dv=HEAD_DIM,
                         q_blk=0, k_blk=c_qw // HEAD_DIM, v_blk=qk_w // HEAD_DIM, mode="bias", bias=sel_bias)

    fbias = _pad_cols(forget_bias.reshape(1, D_HEADS).astype(F32), LANES)
    cum = fox_cumsum(proj_b, blk_fd, fbias, batch, seq)[:, :, :D_HEADS]
    cum_t = cum.transpose(0, 2, 1)
    hd = 4
    d0 = (qk_w + c_kw) // (hd * HEAD_DIM)
    od = flash_attention(pa3, pa3, pa3, n_heads=D_HEADS, hq=hd, hk=hd, dk=HEAD_DIM, dv=HEAD_DIM,
                         q_blk=d0, k_blk=d0 + d_w // (hd * HEAD_DIM), v_blk=d0 + 2 * d_w // (hd * HEAD_DIM),
                         mode="causal", fox=(cum_t[..., None], cum_t[:, :, None, :]))
    return matmul_cat(oc.reshape(T, -1), od.reshape(T, -1), w_out.astype(BF16), F32)


def memory_cross_attention(h, hb, mem_b, batch, seq, w_q, w_kv, w_o, ln_g, ln_b):
    T, D = hb.shape
    mem_w = MEM_HEADS * MEM_HD
    q_scale = MEM_HD ** -0.5 * LOG2E
    q = matmul(hb, (w_q * q_scale).astype(BF16), BF16)
    kv = matmul(mem_b.reshape(-1, D), w_kv.astype(BF16), BF16)
    kv3 = kv.reshape(batch, -1, 2 * mem_w)
    o = flash_attention(q.reshape(batch, seq, mem_w), kv3, kv3, n_heads=MEM_HEADS, hq=MEM_HEADS, hk=MEM_HEADS,
                        dk=MEM_HD, dv=MEM_HD, q_blk=0, k_blk=0, v_blk=1, mode="none", kb=kv3.shape[1])
    return matmul_deepnorm_ln(o.reshape(T, mem_w), w_o.astype(BF16), h, ln_g, ln_b)


def peer_ffn(ht, w_q, sub_keys, u_tabs, v_tabs, layer):
    qt = matmul(w_q.T.astype(BF16), ht, F32)
    sk = sub_keys.reshape(2 * PEER_HEADS, N_KEYS, PEER_DKEY // 2).astype(BF16)
    tables = peer_route(qt, sk)
    u_b, v_b = cast_layer_pair(u_tabs, v_tabs, layer)
    return peer_dense(ht, u_b, v_b, tables)


def kernel(x, mem, ab_w_in, a_rel_bias, b_q_norm, b_w_uq, b_kv_norm, b_w_ukv, ab_w_out, cd_w_in, d_forget_bias,
           cd_w_out, mem_w_q, mem_w_kv, mem_w_o, peer_w_q, peer_sub_keys, peer_u, peer_v, ln_g, ln_b):
    batch, seq, d_model = x.shape
    h = x.reshape(batch * seq, d_model)
    hb = h.astype(BF16)
    mem_b = mem.astype(BF16)
    for layer in range(DEPTH):
        j = layer // 2
        if layer % 2 == 0:
            y = mixer_ab(hb, batch, seq, ab_w_in[j], a_rel_bias[j], b_q_norm[j], b_w_uq[j], b_kv_norm[j],
                         b_w_ukv[j], ab_w_out[j])
        else:
            y = mixer_cd(hb, batch, seq, cd_w_in[j], d_forget_bias[j], cd_w_out[j])
        h, hb = deepnorm_ln(h, y, ln_g[layer, 0], ln_b[layer, 0])
        h, ht = memory_cross_attention(h, hb, mem_b, batch, seq, mem_w_q[layer], mem_w_kv[layer], mem_w_o[layer],
                                       ln_g[layer, 1], ln_b[layer, 1])
        y = peer_ffn(ht, peer_w_q[layer], peer_sub_keys[layer], peer_u, peer_v, layer)
        h, hb = deepnorm_ln(h, y, ln_g[layer, 2], ln_b[layer, 2])
    return h.reshape(batch, seq, d_model)
```
